```python
import jax, jax.numpy as jnp
from jax import lax
import numpy as np

D_MODEL = 1024
BATCH = 8
SEQ = 8192
DEPTH = 2

NORM_EPS = 1e-6
PLE_DIM = 256
D_FF = 4 * D_MODEL
POOL_WINDOWS = (2, 4, 8, 16)
N_POOL_GROUPS = len(POOL_WINDOWS)
POOL_WIDTH = D_MODEL // 2
POOL_GROUP_DIM = POOL_WIDTH // N_POOL_GROUPS
HGRN_WIDTH = D_MODEL // 2
HGRN_HEAD_DIM = 128
HGRN_HEADS = HGRN_WIDTH // HGRN_HEAD_DIM
HGRN_CHUNK = 64
ATTN_HEADS = 16
ATTN_HEAD_DIM = D_MODEL // ATTN_HEADS
DILATION_PAIRS = ((128, 1), (512, 4), (2048, 16))
BLOCK_Q = 128
ROPE_THETA = 10000.0

kernel_name = 'hybrid_pool_hgrn2_dilated_attn_trunk'

F32 = jnp.float32


def _rms(x, g):
    xf = x.astype(F32)
    y = xf * lax.rsqrt(jnp.mean(xf * xf, axis=-1, keepdims=True) + NORM_EPS)
    return (y * g.astype(F32)).astype(x.dtype)


def _multiscale_pool(u, pool_w, pool_scale):
    B_, S_, _ = u.shape
    uf = u.astype(F32).reshape(B_, S_, N_POOL_GROUPS, POOL_GROUP_DIM)
    t = jnp.arange(S_)
    outs = []
    for gi, w in enumerate(POOL_WINDOWS):
        ug = uf[:, :, gi]
        cs = jnp.cumsum(ug, axis=1)
        lag = jnp.pad(cs[:, :S_ - w], ((0, 0), (w, 0), (0, 0)))
        cnt = jnp.minimum(t + 1, w).astype(F32)[None, :, None]
        outs.append((cs - lag) / cnt - ug)
    d = jnp.stack(outs, axis=2).astype(u.dtype)
    y = jnp.einsum('bsgc,gcd->bsgd', d, pool_w)
    return y.reshape(B_, S_, POOL_WIDTH) * pool_scale


def _hgrn2(q, f_pre, i_in, g, lb, o_gain):
    B_, S_, _ = q.shape
    H, K = HGRN_HEADS, HGRN_HEAD_DIM
    C = HGRN_CHUNK
    nC = S_ // C
    f = lb + (1.0 - lb) * jax.nn.sigmoid(f_pre.astype(F32))
    logf = jnp.log(f)
    k = 1.0 - f

    def chunks(a):
        return a.astype(F32).reshape(B_, nC, C, H, K).transpose(1, 0, 3, 2, 4)

    causal = jnp.tril(jnp.ones((C, C), dtype=bool))

    def step(state, inp):
        qc, kc, vc, gc = inp
        b = jnp.cumsum(gc, axis=2)
        o_inter = jnp.einsum('bhtk,bhkv->bhtv', qc * jnp.exp(b), state)
        diff = b[:, :, :, None, :] - b[:, :, None, :, :]
        decay = jnp.exp(jnp.where(causal[:, :, None], diff, -jnp.inf))
        attn = jnp.einsum('bhtk,bhtsk,bhsk->bhts', qc, decay, kc)
        o = o_inter + jnp.einsum('bhts,bhsv->bhtv', attn, vc)
        b_last = b[:, :, -1:, :]
        state = (jnp.exp(b_last[:, :, 0])[..., None] * state
                 + jnp.einsum('bhsk,bhsv->bhkv', kc * jnp.exp(b_last - b), vc))
        return state, o

    init = jnp.zeros((B_, H, K, K), F32)
    _, o = lax.scan(step, init, (chunks(q), chunks(k), chunks(i_in), chunks(logf)))
    o = o.transpose(1, 0, 3, 2, 4).reshape(B_, S_, H, K)
    o = o * lax.rsqrt(jnp.mean(o * o, axis=-1, keepdims=True) + NORM_EPS) * o_gain.astype(F32)
    o = o * jax.nn.silu(g.astype(F32).reshape(B_, S_, H, K))
    return o.reshape(B_, S_, HGRN_WIDTH)


def _rope(x, pos):
    half = x.shape[-1] // 2
    inv = ROPE_THETA ** (-jnp.arange(half, dtype=F32) / half)
    ang = pos[:, None] * inv[None, :]
    cos = jnp.cos(ang)[:, None, :]
    sin = jnp.sin(ang)[:, None, :]
    x1, x2 = x[..., :half], x[..., half:]
    return jnp.concatenate([x1 * cos - x2 * sin, x2 * cos + x1 * sin], axis=-1)


def _dilated_branch(q, k, v, d, w):
    B_, S_, H_, Dh = q.shape
    L = S_ // d

    def to_sub(a):
        return a.reshape(B_, L, d, H_, Dh).transpose(0, 2, 3, 1, 4)

    nb = -(-L // BLOCK_Q)
    Lq = nb * BLOCK_Q
    qs = jnp.pad(to_sub(q), ((0, 0), (0, 0), (0, 0), (0, Lq - L), (0, 0)))
    ks = jnp.pad(to_sub(k), ((0, 0), (0, 0), (0, 0), (w, Lq - L), (0, 0)))
    vs = jnp.pad(to_sub(v), ((0, 0), (0, 0), (0, 0), (w, Lq - L), (0, 0)))
    a_idx = jnp.arange(BLOCK_Q)[:, None]
    c_idx = jnp.arange(BLOCK_Q + w)[None, :]
    dist = a_idx - c_idx + w
    band = (dist >= 0) & (dist <= w)
    scale = Dh ** -0.5

    def block(n):
        start = n * BLOCK_Q
        qb = lax.dynamic_slice_in_dim(qs, start, BLOCK_Q, axis=3)
        kb = lax.dynamic_slice_in_dim(ks, start, BLOCK_Q + w, axis=3)
        vb = lax.dynamic_slice_in_dim(vs, start, BLOCK_Q + w, axis=3)
        valid = band & (start + c_idx - w >= 0)
        s = jnp.einsum('bdhqe,bdhke->bdhqk', qb, kb) * scale
        s = jnp.where(valid, s, -jnp.inf)
        m = jnp.max(s, axis=-1)
        pr = jnp.exp(s - m[..., None])
        l = jnp.sum(pr, axis=-1)
        acc = jnp.einsum('bdhqk,bdhke->bdhqe', pr, vb)
        return m, l, acc

    m, l, acc = lax.map(block, jnp.arange(nb))

    def from_sub(a):
        a = jnp.moveaxis(a, 0, 3)
        a = a.reshape(a.shape[:3] + (Lq,) + a.shape[5:])[:, :, :, :L]
        a = jnp.moveaxis(a, 3, 1)
        return a.reshape((B_, S_, H_) + a.shape[4:])

    return from_sub(m), from_sub(l), from_sub(acc)


def _dilated_attention(q, k, v):
    ms, ls, accs = [], [], []
    for window, dil in DILATION_PAIRS:
        m, l, acc = _dilated_branch(q, k, v, dil, window // dil)
        ms.append(m); ls.append(l); accs.append(acc)
    m_all = ms[0]
    for m in ms[1:]:
        m_all = jnp.maximum(m_all, m)
    num = jnp.zeros_like(accs[0])
    den = jnp.zeros_like(ls[0])
    for m, l, acc in zip(ms, ls, accs):
        wgt = jnp.exp(m - m_all)
        num = num + wgt[..., None] * acc
        den = den + wgt * l
    return num / den[..., None]


def setup_inputs(seed: int = 0) -> dict:
    key = jax.random.key(seed)
    ks = jax.random.split(key, 20)
    n_even = (DEPTH + 1) // 2
    n_odd = DEPTH // 2
    ab_in = POOL_WIDTH + 4 * HGRN_WIDTH
    ab_out = POOL_WIDTH + HGRN_WIDTH
    attn_w = ATTN_HEADS * ATTN_HEAD_DIM

    def nrm(k, shape, scale):
        return jax.random.normal(k, shape, F32) * scale

    def gain(k, shape):
        return 1.0 + 0.1 * jax.random.normal(k, shape, F32)

    return {
        'x': nrm(ks[0], (BATCH, SEQ, D_MODEL), 1.0),
        'p': nrm(ks[1], (DEPTH, BATCH, SEQ, PLE_DIM), 1.0),
        'mix_norm': gain(ks[2], (DEPTH, D_MODEL)),
        'w_in_ab': nrm(ks[3], (n_even, D_MODEL, ab_in), D_MODEL ** -0.5),
        'pool_w': nrm(ks[4], (n_even, N_POOL_GROUPS, POOL_GROUP_DIM, POOL_GROUP_DIM), POOL_GROUP_DIM ** -0.5),
        'pool_scale': gain(ks[5], (n_even, POOL_WIDTH)),
        'hgrn_lb': nrm(ks[6], (DEPTH + 1, HGRN_WIDTH), 0.5),
        'hgrn_o_norm': gain(ks[7], (n_even, HGRN_HEAD_DIM)),
        'w_out_ab': nrm(ks[8], (n_even, ab_out, D_MODEL), ab_out ** -0.5),
        'w_qkv': nrm(ks[9], (n_odd, D_MODEL, 3 * attn_w), D_MODEL ** -0.5),
        'q_norm': gain(ks[10], (n_odd, ATTN_HEAD_DIM)),
        'k_norm': gain(ks[11], (n_odd, ATTN_HEAD_DIM)),
        'w_o': nrm(ks[12], (n_odd, attn_w, D_MODEL), attn_w ** -0.5),
        'mlp_norm': gain(ks[13], (DEPTH, D_MODEL)),
        'w_up': nrm(ks[14], (DEPTH, D_MODEL, D_FF), D_MODEL ** -0.5),
        'w_down': nrm(ks[15], (DEPTH, D_FF, D_MODEL), D_FF ** -0.5),
        'ple_norm': gain(ks[16], (DEPTH, D_MODEL)),
        'w_ple': nrm(ks[17], (DEPTH, PLE_DIM, D_MODEL), PLE_DIM ** -0.5),
        'w_ple_gate': nrm(ks[18], (DEPTH, D_MODEL, D_MODEL), D_MODEL ** -0.5),
    }


def reference(x, p, mix_norm, w_in_ab, pool_w, pool_scale, hgrn_lb, hgrn_o_norm, w_out_ab,
              w_qkv, q_norm, k_norm, w_o, mlp_norm, w_up, w_down, ple_norm, w_ple, w_ple_gate):
    B_, S_, _ = x.shape
    pos = jnp.arange(S_, dtype=F32)
    lb_all = jnp.cumsum(jax.nn.softmax(hgrn_lb.astype(F32), axis=0), axis=0)
    h = x
    for layer in range(DEPTH):
        hn = _rms(h, mix_norm[layer])
        if layer % 2 == 0:
            e = layer // 2
            z = hn @ w_in_ab[e]
            o0 = POOL_WIDTH
            u = z[..., :o0]
            hq = z[..., o0:o0 + HGRN_WIDTH]
            hf = z[..., o0 + HGRN_WIDTH:o0 + 2 * HGRN_WIDTH]
            hi = z[..., o0 + 2 * HGRN_WIDTH:o0 + 3 * HGRN_WIDTH]
            hg = z[..., o0 + 3 * HGRN_WIDTH:]
            a_out = _multiscale_pool(u, pool_w[e], pool_scale[e])
            b_out = _hgrn2(hq, hf, hi, hg, lb_all[layer], hgrn_o_norm[e]).astype(h.dtype)
            mix = jnp.concatenate([a_out.astype(h.dtype), b_out], axis=-1) @ w_out_ab[e]
        else:
            ci = layer // 2
            z = (hn @ w_qkv[ci]).reshape(B_, S_, 3, ATTN_HEADS, ATTN_HEAD_DIM)
            q = _rope(_rms(z[:, :, 0], q_norm[ci]).astype(F32), pos)
            k = _rope(_rms(z[:, :, 1], k_norm[ci]).astype(F32), pos)
            v = z[:, :, 2].astype(F32)
            att = _dilated_attention(q, k, v).reshape(B_, S_, ATTN_HEADS * ATTN_HEAD_DIM)
            mix = att.astype(h.dtype) @ w_o[ci]
        h = h + mix
        hm = _rms(h, mlp_norm[layer])
        h = h + jnp.square(jax.nn.relu(hm @ w_up[layer])) @ w_down[layer]
        gate = jax.nn.sigmoid(_rms(h, ple_norm[layer]) @ w_ple_gate[layer])
        h = h + (p[layer].astype(h.dtype) @ w_ple[layer]) * gate
    return h
```

```python
import functools

import jax
import jax.numpy as jnp
from jax import lax
from jax.experimental import pallas as pl
from jax.experimental.pallas import tpu as pltpu

F32 = jnp.float32
BF16 = jnp.bfloat16

NORM_EPS = 1e-6
POOL_WINDOWS = (2, 4, 8, 16)
POOL_GROUP_DIM = 128
POOL_CARRY_ROWS = 16
HGRN_HEAD_DIM = 128
HGRN_CHUNK = 64
HGRN_DIAG = 16
ATTN_HEAD_DIM = 64
ATTN_WINDOW_STEPS = 128
DILATIONS = (1, 4, 16)
ATTN_BLOCK_Q = 128
ROPE_THETA = 10000.0
LANES = 128
MASK_VALUE = -1e30

VMEM_LIMIT_BYTES = 56 * 1024 * 1024

TOKEN_TILE = 512
ATTN_Q_TILE = 2048


def _dot(a, b):
    return jnp.dot(a, b, preferred_element_type=F32)


def _dot_nt(a, b):
    return lax.dot_general(a, b, (((1,), (1,)), ((), ())), preferred_element_type=F32)


def _dot_tn(a, b):
    return lax.dot_general(a, b, (((0,), (0,)), ((), ())), preferred_element_type=F32)


def _rms(x, g):
    ms = jnp.mean(x * x, axis=-1, keepdims=True)
    return x * lax.rsqrt(ms + NORM_EPS) * g


def _sigmoid(x):
    return 1.0 / (1.0 + jnp.exp(-x))


def _const_spec(shape):
    nd = len(shape)
    return pl.BlockSpec(shape, lambda *_: (0,) * nd, pipeline_mode=pl.Buffered(1))


def _hgrn_diag_block(qb, kb, vb, bb):
    n = qb.shape[0]
    t_idx = lax.broadcasted_iota(jnp.int32, qb.shape, 0)
    acc = jnp.zeros_like(qb)
    for s in range(n):
        diff = jnp.minimum(bb - bb[s:s + 1, :], 0.0)
        e = jnp.where(t_idx >= s, jnp.exp(diff), 0.0)
        a = jnp.sum(qb * e * kb[s:s + 1, :], axis=-1, keepdims=True)
        acc = acc + a * vb[s:s + 1, :]
    return acc


def _hgrn_offdiag(q_rows, b_rows, k_rows, bk_rows, v_rows, anchor):
    qe = (q_rows * jnp.exp(b_rows - anchor)).astype(BF16)
    ke = (k_rows * jnp.exp(anchor - bk_rows)).astype(BF16)
    a = _dot_nt(qe, ke)
    return _dot(a.astype(BF16), v_rows.astype(BF16))


def _hgrn_chunk_head(q, kk, v, b, st):
    c = HGRN_CHUNK
    d = HGRN_DIAG
    o_inter = _dot_nt((q * jnp.exp(b)).astype(BF16), st.astype(BF16))
    pieces = []
    for i in range(c // d):
        rows = slice(i * d, (i + 1) * d)
        pieces.append(_hgrn_diag_block(q[rows], kk[rows], v[rows], b[rows]))
    span = d
    while span < c:
        for start in range(0, c, 2 * span):
            mid = start + span
            anchor = b[mid - 1:mid, :]
            lo = slice(start, mid)
            hi = slice(mid, mid + span)
            contrib = _hgrn_offdiag(q[hi], b[hi], kk[lo], b[lo], v[lo], anchor)
            for j in range(span // d):
                pieces[mid // d + j] = pieces[mid // d + j] + contrib[j * d:(j + 1) * d]
        span *= 2
    o = o_inter + jnp.concatenate(pieces, axis=0)
    b_last = b[c - 1:c, :]
    ke = (kk * jnp.exp(b_last - b)).astype(BF16)
    st_new = jnp.exp(b_last) * st + _dot_tn(v.astype(BF16), ke)
    return o, st_new


def _mixer0_kernel(h_ref, g_ref, win_ref, poolw_ref, pscale_ref, lb_ref, onorm_ref, out_ref,
                   z_ref, upad_ref, state_ref, *, layer):
    t = h_ref.shape[1]
    si = pl.program_id(1)
    pool_w = len(POOL_WINDOWS) * POOL_GROUP_DIM
    n_heads = state_ref.shape[0]
    hw = n_heads * HGRN_HEAD_DIM

    @pl.when(si == 0)
    def _():
        state_ref[...] = jnp.zeros_like(state_ref)
        upad_ref[0:POOL_CARRY_ROWS, :] = jnp.zeros((POOL_CARRY_ROWS, pool_w), F32)

    hn = _rms(h_ref[0], g_ref[...]).astype(BF16)
    z_ref[...] = _dot(hn, win_ref[...])

    upad_ref[POOL_CARRY_ROWS:POOL_CARRY_ROWS + t, :] = z_ref[:, 0:pool_w]
    pos = lax.broadcasted_iota(jnp.int32, (t, 1), 0) + si * t
    for gi, w in enumerate(POOL_WINDOWS):
        cols = slice(gi * POOL_GROUP_DIM, (gi + 1) * POOL_GROUP_DIM)
        u = upad_ref[POOL_CARRY_ROWS:POOL_CARRY_ROWS + t, cols]
        acc = u
        for j in range(1, w):
            acc = acc + upad_ref[POOL_CARRY_ROWS - j:POOL_CARRY_ROWS - j + t, cols]
        cnt = jnp.minimum(pos + 1, w).astype(F32)
        dlt = acc / cnt - u
        y = _dot(dlt.astype(BF16), poolw_ref[gi]) * pscale_ref[:, cols]
        out_ref[0, :, cols] = y.astype(out_ref.dtype)
    upad_ref[0:POOL_CARRY_ROWS, :] = upad_ref[t:t + POOL_CARRY_ROWS, :]

    lbraw = lb_ref[...]
    lbe = jnp.exp(lbraw - jnp.max(lbraw, axis=0, keepdims=True))
    lbs = lbe / jnp.sum(lbe, axis=0, keepdims=True)
    lb = jnp.sum(lbs[0:layer + 1], axis=0, keepdims=True)
    c = HGRN_CHUNK
    ri = lax.broadcasted_iota(jnp.int32, (c, c), 0)
    ci = lax.broadcasted_iota(jnp.int32, (c, c), 1)
    ltri = (ri >= ci).astype(BF16)
    onorm = onorm_ref[...]

    def chunk_body(ck, carry):
        r0 = pl.multiple_of(ck * c, c)
        zq = z_ref[pl.ds(r0, c), pool_w:pool_w + hw]
        zf = z_ref[pl.ds(r0, c), pool_w + hw:pool_w + 2 * hw]
        zi = z_ref[pl.ds(r0, c), pool_w + 2 * hw:pool_w + 3 * hw]
        zg = z_ref[pl.ds(r0, c), pool_w + 3 * hw:pool_w + 4 * hw]
        f = lb + (1.0 - lb) * _sigmoid(zf)
        logf = jnp.log(f)
        kk = 1.0 - f
        g_hi = logf.astype(BF16)
        r1 = logf - g_hi.astype(F32)
        g_mid = r1.astype(BF16)
        g_lo = (r1 - g_mid.astype(F32)).astype(BF16)
        b = _dot(ltri, g_hi) + _dot(ltri, g_mid) + _dot(ltri, g_lo)
        for hd in range(n_heads):
            cols = slice(hd * HGRN_HEAD_DIM, (hd + 1) * HGRN_HEAD_DIM)
            o, st_new = _hgrn_chunk_head(zq[:, cols], kk[:, cols], zi[:, cols], b[:, cols], state_ref[hd])
            state_ref[hd] = st_new
            o = o * lax.rsqrt(jnp.mean(o * o, axis=-1, keepdims=True) + NORM_EPS) * onorm
            gate = zg[:, cols]
            o = o * (gate * _sigmoid(gate))
            out_ref[0, pl.ds(r0, c), pool_w + hd * HGRN_HEAD_DIM:pool_w + (hd + 1) * HGRN_HEAD_DIM] = (
                o.astype(out_ref.dtype))
        return carry

    lax.fori_loop(0, t // c, chunk_body, 0)


def _mixer0(h, g, w_in, pool_w, pool_scale, hgrn_lb, o_norm, layer):
    bsz, s, d = h.shape
    t = min(TOKEN_TILE, s)
    assert s % t == 0 and t % HGRN_CHUNK == 0
    zw = w_in.shape[1]
    pw = len(POOL_WINDOWS) * POOL_GROUP_DIM
    hw = hgrn_lb.shape[1]
    assert zw == pw + 4 * hw and hw % HGRN_HEAD_DIM == 0
    n_heads = hw // HGRN_HEAD_DIM
    return pl.pallas_call(
        functools.partial(_mixer0_kernel, layer=layer),
        grid=(bsz, s // t),
        in_specs=[
            pl.BlockSpec((1, t, d), lambda b, i: (b, i, 0)),
            _const_spec((1, d)),
            _const_spec((d, zw)),
            _const_spec(pool_w.shape),
            _const_spec((1, pw)),
            _const_spec(hgrn_lb.shape),
            _const_spec((1, HGRN_HEAD_DIM)),
        ],
        out_specs=pl.BlockSpec((1, t, pw + hw), lambda b, i: (b, i, 0)),
        out_shape=jax.ShapeDtypeStruct((bsz, s, pw + hw), BF16),
        scratch_shapes=[
            pltpu.VMEM((t, zw), F32),
            pltpu.VMEM((t + POOL_CARRY_ROWS, pw), F32),
            pltpu.VMEM((n_heads, HGRN_HEAD_DIM, HGRN_HEAD_DIM), F32),
        ],
        compiler_params=pltpu.CompilerParams(
            dimension_semantics=("arbitrary", "arbitrary"), vmem_limit_bytes=VMEM_LIMIT_BYTES),
        name="mixer0",
    )(h, g.reshape(1, d), w_in.astype(BF16), pool_w.astype(BF16), pool_scale.reshape(1, pw),
      hgrn_lb, o_norm.reshape(1, HGRN_HEAD_DIM))


def _post_kernel(h_ref, mix_ref, p_ref, wmix_ref, gm_ref, wup_ref, wdn_ref, gp_ref, wg_ref, wple_ref, out_ref,
                 *, ff_chunk):
    h1 = h_ref[0] + _dot(mix_ref[0], wmix_ref[...])
    hm = _rms(h1, gm_ref[...]).astype(BF16)
    d_ff = wup_ref.shape[1]
    acc = jnp.zeros_like(h1)
    for c0 in range(0, d_ff, ff_chunk):
        up = _dot(hm, wup_ref[:, c0:c0 + ff_chunk])
        act = jnp.square(jnp.maximum(up, 0.0)).astype(BF16)
        acc = acc + _dot(act, wdn_ref[c0:c0 + ff_chunk, :])
    h2 = h1 + acc
    gate = _sigmoid(_dot(_rms(h2, gp_ref[...]).astype(BF16), wg_ref[...]))
    pe = _dot(p_ref[0].astype(BF16), wple_ref[...])
    out_ref[0] = h2 + pe * gate


def _post(h, mix, p, w_mix, g_mlp, w_up, w_down, g_ple, w_gate, w_ple):
    bsz, s, d = h.shape
    t = min(TOKEN_TILE, s)
    assert s % t == 0
    d_mix = mix.shape[-1]
    d_ff = w_up.shape[1]
    d_p = p.shape[-1]
    ff_chunk = min(1024, d_ff)
    assert d_ff % ff_chunk == 0
    tok = lambda width: pl.BlockSpec((1, t, width), lambda b, i: (b, i, 0))
    return pl.pallas_call(
        functools.partial(_post_kernel, ff_chunk=ff_chunk),
        grid=(bsz, s // t),
        in_specs=[
            tok(d), tok(d_mix), tok(d_p),
            _const_spec((d_mix, d)),
            _const_spec((1, d)),
            _const_spec((d, d_ff)),
            _const_spec((d_ff, d)),
            _const_spec((1, d)),
            _const_spec((d, d)),
            _const_spec((d_p, d)),
        ],
        out_specs=tok(d),
        out_shape=jax.ShapeDtypeStruct((bsz, s, d), F32),
        compiler_params=pltpu.CompilerParams(
            dimension_semantics=("arbitrary", "arbitrary"), vmem_limit_bytes=VMEM_LIMIT_BYTES),
        name="post",
    )(h, mix, p, w_mix.astype(BF16), g_mlp.reshape(1, d), w_up.astype(BF16), w_down.astype(BF16),
      g_ple.reshape(1, d), w_gate.astype(BF16), w_ple.astype(BF16))


def _rope_kernel(inv_ref, cos_ref, sin_ref):
    t = cos_ref.shape[0]
    pos = (lax.broadcasted_iota(jnp.int32, (t, LANES), 0) + pl.program_id(0) * t).astype(F32)
    lane = lax.broadcasted_iota(jnp.int32, (t, LANES), 1)
    ang = pos * inv_ref[...]
    cos_ref[...] = jnp.cos(ang)
    sin = jnp.sin(ang)
    half = ATTN_HEAD_DIM // 2
    sin_ref[...] = jnp.where(lane % ATTN_HEAD_DIM < half, -sin, sin)


def _rope_tables(s):
    half = ATTN_HEAD_DIM // 2
    inv = ROPE_THETA ** (-jnp.arange(half, dtype=F32) / half)
    inv = jnp.tile(inv, LANES // half).reshape(1, LANES)
    t = min(1024, s)
    assert s % t == 0
    return pl.pallas_call(
        _rope_kernel,
        grid=(s // t,),
        in_specs=[pl.BlockSpec((1, LANES), lambda i: (0, 0))],
        out_specs=[pl.BlockSpec((t, LANES), lambda i: (i, 0))] * 2,
        out_shape=[jax.ShapeDtypeStruct((s, LANES), F32)] * 2,
        name="rope_tables",
    )(inv)


def _qkv_kernel(h_ref, g_ref, w_ref, qg_ref, kg_ref, cos_ref, sin_ref, sel_ref, selt_ref,
                q_ref, k_ref, v_ref, *, q_scale):
    d = h_ref.shape[2]
    n_pairs = q_ref.shape[1]
    hn = _rms(h_ref[0], g_ref[...]).astype(BF16)
    z = _dot(hn, w_ref[...])
    cos = cos_ref[...]
    sin = sin_ref[...]
    lane = lax.broadcasted_iota(jnp.int32, cos.shape, 1)
    first_half = lane % ATTN_HEAD_DIM < ATTN_HEAD_DIM // 2

    def norm_rope(x, gain, out_ref, scale):
        ss = _dot((x * x).astype(BF16), sel_ref[...])
        inv = lax.rsqrt(ss * (1.0 / ATTN_HEAD_DIM) + NORM_EPS) * scale
        inv_hi = inv.astype(BF16)
        inv_lo = (inv - inv_hi.astype(F32)).astype(BF16)
        invb = _dot(inv_hi, selt_ref[...]) + _dot(inv_lo, selt_ref[...])
        xn = x * invb * gain
        for hp in range(n_pairs):
            xb = xn[:, hp * LANES:(hp + 1) * LANES]
            rot = jnp.where(first_half, pltpu.roll(xb, LANES - ATTN_HEAD_DIM // 2, 1),
                            pltpu.roll(xb, ATTN_HEAD_DIM // 2, 1))
            out_ref[0, hp] = xb * cos + rot * sin

    norm_rope(z[:, 0:d], qg_ref[...], q_ref, q_scale)
    norm_rope(z[:, d:2 * d], kg_ref[...], k_ref, 1.0)
    for hp in range(n_pairs):
        v_ref[0, hp] = z[:, 2 * d + hp * LANES:2 * d + (hp + 1) * LANES]


def _qkv(h, g, w_qkv, q_gain, k_gain, cos, sin):
    bsz, s, d = h.shape
    t = min(TOKEN_TILE, s)
    assert s % t == 0 and w_qkv.shape[1] == 3 * d and d % LANES == 0
    n_heads = d // ATTN_HEAD_DIM
    n_pairs = d // LANES
    head_of = jnp.arange(d) // ATTN_HEAD_DIM
    sel = (head_of[:, None] == jnp.arange(LANES)[None, :]).astype(BF16)
    out_sds = jax.ShapeDtypeStruct((bsz, n_pairs, s, LANES), F32)
    out_spec = pl.BlockSpec((1, n_pairs, t, LANES), lambda b, i: (b, 0, i, 0))
    return pl.pallas_call(
        functools.partial(_qkv_kernel, q_scale=ATTN_HEAD_DIM ** -0.5),
        grid=(bsz, s // t),
        in_specs=[
            pl.BlockSpec((1, t, d), lambda b, i: (b, i, 0)),
            _const_spec((1, d)),
            _const_spec((d, 3 * d)),
            _const_spec((1, d)),
            _const_spec((1, d)),
            pl.BlockSpec((t, LANES), lambda b, i: (i, 0)),
            pl.BlockSpec((t, LANES), lambda b, i: (i, 0)),
            _const_spec((d, LANES)),
            _const_spec((LANES, d)),
        ],
        out_specs=[out_spec] * 3,
        out_shape=[out_sds] * 3,
        compiler_params=pltpu.CompilerParams(
            dimension_semantics=("arbitrary", "arbitrary"), vmem_limit_bytes=VMEM_LIMIT_BYTES),
        name="qkv",
    )(h, g.reshape(1, d), w_qkv.astype(BF16), jnp.tile(q_gain, n_heads).reshape(1, d),
      jnp.tile(k_gain, n_heads).reshape(1, d), cos, sin, sel, sel.T)


def _attn_kernel(q_ref, k_ref, v_ref, out_ref, acc_ref, m_ref, l_ref):
    tq = q_ref.shape[2]
    bq = ATTN_BLOCK_Q
    w = ATTN_WINDOW_STEPS
    nk = bq + w
    t0 = pl.program_id(2) * tq
    lane = lax.broadcasted_iota(jnp.int32, (bq, LANES), 1)
    head0 = lane < ATTN_HEAD_DIM
    i_minus_j = (lax.broadcasted_iota(jnp.int32, (bq, nk), 0) - lax.broadcasted_iota(jnp.int32, (bq, nk), 1))

    for br, dil in enumerate(DILATIONS):
        blocks_per_res = tq // (dil * bq)

        def block_body(idx, carry, br=br, dil=dil, blocks_per_res=blocks_per_res):
            res = idx // blocks_per_res
            blk = idx % blocks_per_res
            q_start = res + dil * bq * blk
            k_nominal = t0 + q_start - dil * w
            k_start = jnp.maximum(k_nominal, res)
            koff = (t0 + q_start - k_start) // dil
            qb = q_ref[0, 0, pl.ds(q_start, bq, stride=dil), :]
            kb = k_ref[0, 0, pl.ds(k_start, nk, stride=dil), :].astype(BF16)
            vb = v_ref[0, 0, pl.ds(k_start, nk, stride=dil), :].astype(BF16)
            qq = jnp.concatenate([jnp.where(head0, qb, 0.0), jnp.where(head0, 0.0, qb)], axis=0).astype(BF16)
            s = _dot_nt(qq, kb)
            dist = i_minus_j + koff
            bias = jnp.where((dist >= 0) & (dist <= w), 0.0, MASK_VALUE)
            s = s + jnp.concatenate([bias, bias], axis=0)
            m = jnp.max(s, axis=-1, keepdims=True)
            p = jnp.exp(s - m)
            l = jnp.sum(p, axis=-1, keepdims=True)
            pv = _dot(p.astype(BF16), vb)
            rows = pl.ds(q_start, bq, stride=dil)
            acc_ref[br, rows, :] = jnp.where(head0, pv[:bq], pv[bq:])
            m_ref[br, rows, :] = jnp.where(head0, m[:bq], m[bq:])
            l_ref[br, rows, :] = jnp.where(head0, l[:bq], l[bq:])
            return carry

        lax.fori_loop(0, tq // bq, block_body, 0)

    m_all = jnp.max(m_ref[...], axis=0)
    num = jnp.zeros((tq, LANES), F32)
    den = jnp.zeros((tq, LANES), F32)
    for br in range(len(DILATIONS)):
        wgt = jnp.exp(m_ref[br] - m_all)
        num = num + wgt * acc_ref[br]
        den = den + wgt * l_ref[br]
    out_ref[0] = (num / den).astype(out_ref.dtype)


def _attention(q, k, v):
    bsz, n_pairs, s, _ = q.shape
    tq = ATTN_Q_TILE
    assert s % tq == 0 and tq == ATTN_BLOCK_Q * max(DILATIONS)
    nb = len(DILATIONS)
    q_spec = pl.BlockSpec((1, 1, tq, LANES), lambda b, hp, i: (b, hp, i, 0))
    kv_spec = pl.BlockSpec((1, 1, s, LANES), lambda b, hp, i: (b, hp, 0, 0))
    return pl.pallas_call(
        _attn_kernel,
        grid=(bsz, n_pairs, s // tq),
        in_specs=[q_spec, kv_spec, kv_spec],
        out_specs=pl.BlockSpec((1, tq, LANES), lambda b, hp, i: (b, i, hp)),
        out_shape=jax.ShapeDtypeStruct((bsz, s, n_pairs * LANES), BF16),
        scratch_shapes=[pltpu.VMEM((nb, tq, LANES), F32)] * 3,
        compiler_params=pltpu.CompilerParams(
            dimension_semantics=("arbitrary", "arbitrary", "arbitrary"), vmem_limit_bytes=VMEM_LIMIT_BYTES),
        name="dilated_attention",
    )(q, k, v)


def kernel(x, p, mix_norm, w_in_ab, pool_w, pool_scale, hgrn_lb, hgrn_o_norm, w_out_ab, w_qkv, q_norm, k_norm,
           w_o, mlp_norm, w_up, w_down, ple_norm, w_ple, w_ple_gate):
    depth = p.shape[0]
    s = x.shape[1]
    h = x
    cos = sin = None
    for layer in range(depth):
        if layer % 2 == 0:
            e = layer // 2
            mix = _mixer0(h, mix_norm[layer], w_in_ab[e], pool_w[e], pool_scale[e], hgrn_lb, hgrn_o_norm[e], layer)
            w_mix = w_out_ab[e]
        else:
            ci = layer // 2
            if cos is None:
                cos, sin = _rope_tables(s)
            q, k, v = _qkv(h, mix_norm[layer], w_qkv[ci], q_norm[ci], k_norm[ci], cos, sin)
            mix = _attention(q, k, v)
            w_mix = w_o[ci]
        h = _post(h, mix, p[layer], w_mix, mlp_norm[layer], w_up[layer], w_down[layer], ple_norm[layer],
                  w_ple_gate[layer], w_ple[layer])
    return h
```

```python
import functools

import jax
import jax.numpy as jnp
from jax import lax
from jax.experimental import pallas as pl
from jax.experimental.pallas import tpu as pltpu

F32 = jnp.float32
BF16 = jnp.bfloat16

NORM_EPS = 1e-6
POOL_WINDOWS = (2, 4, 8, 16)
POOL_GROUP_DIM = 128
POOL_CARRY_ROWS = 16
HGRN_HEAD_DIM = 128
HGRN_CHUNK = 64
HGRN_DIAG = 16
ATTN_HEAD_DIM = 64
ATTN_WINDOW_STEPS = 128
DILATIONS = (1, 4, 16)
ATTN_BLOCK_Q = 128
ROPE_THETA = 10000.0
LANES = 128
MASK_VALUE = -1e30
LOG2_E = 1.4426950408889634

VMEM_LIMIT_BYTES = 56 * 1024 * 1024

TOKEN_TILE = 512
ATTN_Q_TILE = 2048
ATTN_UNROLL = 16
ATTN_RELAYOUT_ROWS = 256


def _dot(a, b):
    return jnp.dot(a, b, preferred_element_type=F32)


def _dot_nt(a, b):
    return lax.dot_general(a, b, (((1,), (1,)), ((), ())), preferred_element_type=F32)


def _dot_tn(a, b):
    return lax.dot_general(a, b, (((0,), (0,)), ((), ())), preferred_element_type=F32)


def _rms(x, g):
    ms = jnp.mean(x * x, axis=-1, keepdims=True)
    return x * lax.rsqrt(ms + NORM_EPS) * g


def _sigmoid(x):
    return 1.0 / (1.0 + jnp.exp(-x))


def _const_spec(shape):
    nd = len(shape)
    return pl.BlockSpec(shape, lambda *_: (0,) * nd, pipeline_mode=pl.Buffered(1))


def _hgrn_diag_block(qb, kb, vb, bb):
    n = qb.shape[0]
    t_idx = lax.broadcasted_iota(jnp.int32, qb.shape, 0)
    acc = jnp.zeros_like(qb)
    for s in range(n):
        diff = jnp.minimum(bb - bb[s:s + 1, :], 0.0)
        e = jnp.where(t_idx >= s, jnp.exp(diff), 0.0)
        a = jnp.sum(qb * e * kb[s:s + 1, :], axis=-1, keepdims=True)
        acc = acc + a * vb[s:s + 1, :]
    return acc


def _hgrn_offdiag(q_rows, b_rows, k_rows, bk_rows, v_rows, anchor):
    qe = (q_rows * jnp.exp(b_rows - anchor)).astype(BF16)
    ke = (k_rows * jnp.exp(anchor - bk_rows)).astype(BF16)
    a = _dot_nt(qe, ke)
    return _dot(a.astype(BF16), v_rows.astype(BF16))


def _hgrn_chunk_head(q, kk, v, b, st):
    c = HGRN_CHUNK
    d = HGRN_DIAG
    o_inter = _dot_nt((q * jnp.exp(b)).astype(BF16), st.astype(BF16))
    pieces = []
    for i in range(c // d):
        rows = slice(i * d, (i + 1) * d)
        pieces.append(_hgrn_diag_block(q[rows], kk[rows], v[rows], b[rows]))
    span = d
    while span < c:
        for start in range(0, c, 2 * span):
            mid = start + span
            anchor = b[mid - 1:mid, :]
            lo = slice(start, mid)
            hi = slice(mid, mid + span)
            contrib = _hgrn_offdiag(q[hi], b[hi], kk[lo], b[lo], v[lo], anchor)
            for j in range(span // d):
                pieces[mid // d + j] = pieces[mid // d + j] + contrib[j * d:(j + 1) * d]
        span *= 2
    o = o_inter + jnp.concatenate(pieces, axis=0)
    b_last = b[c - 1:c, :]
    ke = (kk * jnp.exp(b_last - b)).astype(BF16)
    st_new = jnp.exp(b_last) * st + _dot_tn(v.astype(BF16), ke)
    return o, st_new


def _mixer0_kernel(h_ref, g_ref, win_ref, poolw_ref, pscale_ref, lb_ref, onorm_ref, out_ref,
                   z_ref, upad_ref, state_ref, *, layer):
    t = h_ref.shape[1]
    si = pl.program_id(1)
    pool_w = len(POOL_WINDOWS) * POOL_GROUP_DIM
    n_heads = state_ref.shape[0]
    hw = n_heads * HGRN_HEAD_DIM

    @pl.when(si == 0)
    def _():
        state_ref[...] = jnp.zeros_like(state_ref)
        upad_ref[0:POOL_CARRY_ROWS, :] = jnp.zeros((POOL_CARRY_ROWS, pool_w), F32)

    hn = _rms(h_ref[0], g_ref[...]).astype(BF16)
    z_ref[...] = _dot(hn, win_ref[...])

    upad_ref[POOL_CARRY_ROWS:POOL_CARRY_ROWS + t, :] = z_ref[:, 0:pool_w]
    pos = lax.broadcasted_iota(jnp.int32, (t, 1), 0) + si * t
    for gi, w in enumerate(POOL_WINDOWS):
        cols = slice(gi * POOL_GROUP_DIM, (gi + 1) * POOL_GROUP_DIM)
        u = upad_ref[POOL_CARRY_ROWS:POOL_CARRY_ROWS + t, cols]
        acc = u
        for j in range(1, w):
            acc = acc + upad_ref[POOL_CARRY_ROWS - j:POOL_CARRY_ROWS - j + t, cols]
        cnt = jnp.minimum(pos + 1, w).astype(F32)
        dlt = acc / cnt - u
        y = _dot(dlt.astype(BF16), poolw_ref[gi]) * pscale_ref[:, cols]
        out_ref[0, :, cols] = y.astype(out_ref.dtype)
    upad_ref[0:POOL_CARRY_ROWS, :] = upad_ref[t:t + POOL_CARRY_ROWS, :]

    lbraw = lb_ref[...]
    lbe = jnp.exp(lbraw - jnp.max(lbraw, axis=0, keepdims=True))
    lbs = lbe / jnp.sum(lbe, axis=0, keepdims=True)
    lb = jnp.sum(lbs[0:layer + 1], axis=0, keepdims=True)
    c = HGRN_CHUNK
    ri = lax.broadcasted_iota(jnp.int32, (c, c), 0)
    ci = lax.broadcasted_iota(jnp.int32, (c, c), 1)
    ltri = (ri >= ci).astype(BF16)
    onorm = onorm_ref[...]

    def chunk_body(ck, carry):
        r0 = pl.multiple_of(ck * c, c)
        zq = z_ref[pl.ds(r0, c), pool_w:pool_w + hw]
        zf = z_ref[pl.ds(r0, c), pool_w + hw:pool_w + 2 * hw]
        zi = z_ref[pl.ds(r0, c), pool_w + 2 * hw:pool_w + 3 * hw]
        zg = z_ref[pl.ds(r0, c), pool_w + 3 * hw:pool_w + 4 * hw]
        f = lb + (1.0 - lb) * _sigmoid(zf)
        logf = jnp.log(f)
        kk = 1.0 - f
        g_hi = logf.astype(BF16)
        r1 = logf - g_hi.astype(F32)
        g_mid = r1.astype(BF16)
        g_lo = (r1 - g_mid.astype(F32)).astype(BF16)
        b = _dot(ltri, g_hi) + _dot(ltri, g_mid) + _dot(ltri, g_lo)
        for hd in range(n_heads):
            cols = slice(hd * HGRN_HEAD_DIM, (hd + 1) * HGRN_HEAD_DIM)
            o, st_new = _hgrn_chunk_head(zq[:, cols], kk[:, cols], zi[:, cols], b[:, cols], state_ref[hd])
            state_ref[hd] = st_new
            o = o * lax.rsqrt(jnp.mean(o * o, axis=-1, keepdims=True) + NORM_EPS) * onorm
            gate = zg[:, cols]
            o = o * (gate * _sigmoid(gate))
            out_ref[0, pl.ds(r0, c), pool_w + hd * HGRN_HEAD_DIM:pool_w + (hd + 1) * HGRN_HEAD_DIM] = (
                o.astype(out_ref.dtype))
        return carry

    lax.fori_loop(0, t // c, chunk_body, 0)


def _mixer0(h, g, w_in, pool_w, pool_scale, hgrn_lb, o_norm, layer):
    bsz, s, d = h.shape
    t = min(TOKEN_TILE, s)
    assert s % t == 0 and t % HGRN_CHUNK == 0
    zw = w_in.shape[1]
    pw = len(POOL_WINDOWS) * POOL_GROUP_DIM
    hw = hgrn_lb.shape[1]
    assert zw == pw + 4 * hw and hw % HGRN_HEAD_DIM == 0
    n_heads = hw // HGRN_HEAD_DIM
    return pl.pallas_call(
        functools.partial(_mixer0_kernel, layer=layer),
        grid=(bsz, s // t),
        in_specs=[
            pl.BlockSpec((1, t, d), lambda b, i: (b, i, 0)),
            _const_spec((1, d)),
            _const_spec((d, zw)),
            _const_spec(pool_w.shape),
            _const_spec((1, pw)),
            _const_spec(hgrn_lb.shape),
            _const_spec((1, HGRN_HEAD_DIM)),
        ],
        out_specs=pl.BlockSpec((1, t, pw + hw), lambda b, i: (b, i, 0)),
        out_shape=jax.ShapeDtypeStruct((bsz, s, pw + hw), BF16),
        scratch_shapes=[
            pltpu.VMEM((t, zw), F32),
            pltpu.VMEM((t + POOL_CARRY_ROWS, pw), F32),
            pltpu.VMEM((n_heads, HGRN_HEAD_DIM, HGRN_HEAD_DIM), F32),
        ],
        compiler_params=pltpu.CompilerParams(
            dimension_semantics=("arbitrary", "arbitrary"), vmem_limit_bytes=VMEM_LIMIT_BYTES),
        name="mixer0",
    )(h, g.reshape(1, d), w_in.astype(BF16), pool_w.astype(BF16), pool_scale.reshape(1, pw),
      hgrn_lb, o_norm.reshape(1, HGRN_HEAD_DIM))


def _post_kernel(h_ref, mix_ref, p_ref, wmix_ref, gm_ref, wup_ref, wdn_ref, gp_ref, wg_ref, wple_ref, out_ref,
                 *, ff_chunk):
    h1 = h_ref[0] + _dot(mix_ref[0], wmix_ref[...])
    hm = _rms(h1, gm_ref[...]).astype(BF16)
    d_ff = wup_ref.shape[1]
    acc = jnp.zeros_like(h1)
    for c0 in range(0, d_ff, ff_chunk):
        up = _dot(hm, wup_ref[:, c0:c0 + ff_chunk])
        act = jnp.square(jnp.maximum(up, 0.0)).astype(BF16)
        acc = acc + _dot(act, wdn_ref[c0:c0 + ff_chunk, :])
    h2 = h1 + acc
    gate = _sigmoid(_dot(_rms(h2, gp_ref[...]).astype(BF16), wg_ref[...]))
    pe = _dot(p_ref[0].astype(BF16), wple_ref[...])
    out_ref[0] = h2 + pe * gate


def _post(h, mix, p, w_mix, g_mlp, w_up, w_down, g_ple, w_gate, w_ple):
    bsz, s, d = h.shape
    t = min(TOKEN_TILE, s)
    assert s % t == 0
    d_mix = mix.shape[-1]
    d_ff = w_up.shape[1]
    d_p = p.shape[-1]
    ff_chunk = min(1024, d_ff)
    assert d_ff % ff_chunk == 0
    tok = lambda width: pl.BlockSpec((1, t, width), lambda b, i: (b, i, 0))
    return pl.pallas_call(
        functools.partial(_post_kernel, ff_chunk=ff_chunk),
        grid=(bsz, s // t),
        in_specs=[
            tok(d), tok(d_mix), tok(d_p),
            _const_spec((d_mix, d)),
            _const_spec((1, d)),
            _const_spec((d, d_ff)),
            _const_spec((d_ff, d)),
            _const_spec((1, d)),
            _const_spec((d, d)),
            _const_spec((d_p, d)),
        ],
        out_specs=tok(d),
        out_shape=jax.ShapeDtypeStruct((bsz, s, d), F32),
        compiler_params=pltpu.CompilerParams(
            dimension_semantics=("arbitrary", "arbitrary"), vmem_limit_bytes=VMEM_LIMIT_BYTES),
        name="post",
    )(h, mix, p, w_mix.astype(BF16), g_mlp.reshape(1, d), w_up.astype(BF16), w_down.astype(BF16),
      g_ple.reshape(1, d), w_gate.astype(BF16), w_ple.astype(BF16))


def _rope_kernel(inv_ref, cos_ref, sin_ref):
    t = cos_ref.shape[0]
    pos = (lax.broadcasted_iota(jnp.int32, (t, LANES), 0) + pl.program_id(0) * t).astype(F32)
    lane = lax.broadcasted_iota(jnp.int32, (t, LANES), 1)
    ang = pos * inv_ref[...]
    cos_ref[...] = jnp.cos(ang)
    sin = jnp.sin(ang)
    half = ATTN_HEAD_DIM // 2
    sin_ref[...] = jnp.where(lane % ATTN_HEAD_DIM < half, -sin, sin)


def _rope_tables(s):
    half = ATTN_HEAD_DIM // 2
    inv = ROPE_THETA ** (-jnp.arange(half, dtype=F32) / half)
    inv = jnp.tile(inv, LANES // half).reshape(1, LANES)
    t = min(1024, s)
    assert s % t == 0
    return pl.pallas_call(
        _rope_kernel,
        grid=(s // t,),
        in_specs=[pl.BlockSpec((1, LANES), lambda i: (0, 0))],
        out_specs=[pl.BlockSpec((t, LANES), lambda i: (i, 0))] * 2,
        out_shape=[jax.ShapeDtypeStruct((s, LANES), F32)] * 2,
        name="rope_tables",
    )(inv)


def _qkv_kernel(h_ref, g_ref, w_ref, qg_ref, kg_ref, cos_ref, sin_ref, sel_ref, selt_ref,
                q_ref, k_ref, v_ref, *, q_scale):
    d = h_ref.shape[2]
    n_pairs = q_ref.shape[1]
    hn = _rms(h_ref[0], g_ref[...]).astype(BF16)
    z = _dot(hn, w_ref[...])
    cos = cos_ref[...]
    sin = sin_ref[...]
    lane = lax.broadcasted_iota(jnp.int32, cos.shape, 1)
    first_half = lane % ATTN_HEAD_DIM < ATTN_HEAD_DIM // 2

    def norm_rope(x, gain, out_ref, scale):
        ss = _dot((x * x).astype(BF16), sel_ref[...])
        inv = lax.rsqrt(ss * (1.0 / ATTN_HEAD_DIM) + NORM_EPS) * scale
        inv_hi = inv.astype(BF16)
        inv_lo = (inv - inv_hi.astype(F32)).astype(BF16)
        invb = _dot(inv_hi, selt_ref[...]) + _dot(inv_lo, selt_ref[...])
        xn = x * invb * gain
        for hp in range(n_pairs):
            xb = xn[:, hp * LANES:(hp + 1) * LANES]
            rot = jnp.where(first_half, pltpu.roll(xb, LANES - ATTN_HEAD_DIM // 2, 1),
                            pltpu.roll(xb, ATTN_HEAD_DIM // 2, 1))
            out_ref[0, hp] = xb * cos + rot * sin

    norm_rope(z[:, 0:d], qg_ref[...], q_ref, q_scale)
    norm_rope(z[:, d:2 * d], kg_ref[...], k_ref, 1.0)
    for hp in range(n_pairs):
        v_ref[0, hp] = z[:, 2 * d + hp * LANES:2 * d + (hp + 1) * LANES]


def _qkv(h, g, w_qkv, q_gain, k_gain, cos, sin):
    bsz, s, d = h.shape
    t = min(TOKEN_TILE, s)
    assert s % t == 0 and w_qkv.shape[1] == 3 * d and d % LANES == 0
    n_heads = d // ATTN_HEAD_DIM
    n_pairs = d // LANES
    head_of = jnp.arange(d) // ATTN_HEAD_DIM
    sel = (head_of[:, None] == jnp.arange(LANES)[None, :]).astype(BF16)
    out_sds = jax.ShapeDtypeStruct((bsz, n_pairs, s, LANES), F32)
    out_spec = pl.BlockSpec((1, n_pairs, t, LANES), lambda b, i: (b, 0, i, 0))
    return pl.pallas_call(
        functools.partial(_qkv_kernel, q_scale=ATTN_HEAD_DIM ** -0.5 * LOG2_E),
        grid=(bsz, s // t),
        in_specs=[
            pl.BlockSpec((1, t, d), lambda b, i: (b, i, 0)),
            _const_spec((1, d)),
            _const_spec((d, 3 * d)),
            _const_spec((1, d)),
            _const_spec((1, d)),
            pl.BlockSpec((t, LANES), lambda b, i: (i, 0)),
            pl.BlockSpec((t, LANES), lambda b, i: (i, 0)),
            _const_spec((d, LANES)),
            _const_spec((LANES, d)),
        ],
        out_specs=[out_spec] * 3,
        out_shape=[out_sds] * 3,
        compiler_params=pltpu.CompilerParams(
            dimension_semantics=("arbitrary", "arbitrary"), vmem_limit_bytes=VMEM_LIMIT_BYTES),
        name="qkv",
    )(h, g.reshape(1, d), w_qkv.astype(BF16), jnp.tile(q_gain, n_heads).reshape(1, d),
      jnp.tile(k_gain, n_heads).reshape(1, d), cos, sin, sel, sel.T)


def _dilated_relayout(src_ref, dst_ref, tmp_ref):
    s_len = src_ref.shape[2]
    rows = ATTN_RELAYOUT_ROWS
    d1, d2 = DILATIONS[1], DILATIONS[2]
    ratio = d2 // d1

    def natural(c, carry):
        r0 = pl.multiple_of(c * rows, rows)
        dst_ref[0, pl.ds(r0, rows), :] = src_ref[0, 0, pl.ds(r0, rows), :].astype(BF16)
        return carry

    lax.fori_loop(0, s_len // rows, natural, 0)

    per_res1 = s_len // d1 // rows

    def level1(idx, carry):
        res = idx // per_res1
        c = idx % per_res1
        x = src_ref[0, 0, pl.ds(res + d1 * rows * c, rows, stride=d1), :]
        o0 = pl.multiple_of(res * (s_len // d1) + rows * c, rows)
        tmp_ref[pl.ds(o0, rows), :] = x
        dst_ref[1, pl.ds(o0, rows), :] = x.astype(BF16)
        return carry

    lax.fori_loop(0, d1 * per_res1, level1, 0)

    per_res2 = s_len // d2 // rows

    def level2(idx, carry):
        res = idx // per_res2
        c = idx % per_res2
        src0 = (res % d1) * (s_len // d1) + res // d1 + ratio * rows * c
        x = tmp_ref[pl.ds(src0, rows, stride=ratio), :]
        o0 = pl.multiple_of(res * (s_len // d2) + rows * c, rows)
        dst_ref[2, pl.ds(o0, rows), :] = x.astype(BF16)
        return carry

    lax.fori_loop(0, d2 * per_res2, level2, 0)


def _attn_kernel(q_ref, k_ref, v_ref, out_ref, ks_ref, vs_ref, tmp_ref, bias_ref, acc_ref, m_ref, l_ref):
    s_len = k_ref.shape[2]
    tq = q_ref.shape[2]
    bq = ATTN_BLOCK_Q
    w = ATTN_WINDOW_STEPS
    nk = bq + w
    qi = pl.program_id(2)
    t0 = qi * tq

    @pl.when(qi == 0)
    def _():
        _dilated_relayout(k_ref, ks_ref, tmp_ref)
        _dilated_relayout(v_ref, vs_ref, tmp_ref)

    i_minus_j = (lax.broadcasted_iota(jnp.int32, (bq, nk), 0) - lax.broadcasted_iota(jnp.int32, (bq, nk), 1))
    for variant, koff in enumerate((w, 0)):
        dist = i_minus_j + koff
        bias_ref[variant] = jnp.where((dist >= 0) & (dist <= w), 0.0, MASK_VALUE)

    lane = lax.broadcasted_iota(jnp.int32, (bq, LANES), 1)
    head0 = lane < ATTN_HEAD_DIM
    ones = jnp.ones((nk, LANES), BF16)

    for br, dil in enumerate(DILATIONS):
        blocks_per_res = tq // (dil * bq)

        def block_body(idx, carry, br=br, dil=dil, blocks_per_res=blocks_per_res):
            res = idx // blocks_per_res
            blk = idx % blocks_per_res
            q_start = res + dil * bq * blk
            l_k0 = t0 // dil + bq * blk - w
            variant = (l_k0 < 0).astype(jnp.int32)
            row0 = pl.multiple_of(res * (s_len // dil) + jnp.maximum(l_k0, 0), bq)
            qb = q_ref[0, 0, pl.ds(q_start, bq, stride=dil), :]
            kb = ks_ref[br, pl.ds(row0, nk), :]
            vb = vs_ref[br, pl.ds(row0, nk), :]
            qq = jnp.concatenate([jnp.where(head0, qb, 0.0), jnp.where(head0, 0.0, qb)], axis=0).astype(BF16)
            s = _dot_nt(qq, kb)
            bias = bias_ref[variant]
            s = jnp.concatenate([s[:bq] + bias, s[bq:] + bias], axis=0)
            m = jnp.max(s, axis=-1, keepdims=True)
            p = jnp.exp2(s - m).astype(BF16)
            pv = _dot(p, jnp.concatenate([vb, ones], axis=1))
            rows = pl.ds(q_start, bq, stride=dil)
            acc_ref[br, rows, :] = jnp.where(head0, pv[:bq, :LANES], pv[bq:, :LANES])
            l_ref[br, rows, :] = jnp.where(head0, pv[:bq, LANES:], pv[bq:, LANES:])
            m_ref[br, rows, :] = jnp.where(head0, m[:bq], m[bq:])
            return carry

        lax.fori_loop(0, tq // bq, block_body, 0, unroll=ATTN_UNROLL)

    m_all = jnp.max(m_ref[...], axis=0)
    num = jnp.zeros((tq, LANES), F32)
    den = jnp.zeros((tq, LANES), F32)
    for br in range(len(DILATIONS)):
        wgt = jnp.exp2(m_ref[br] - m_all)
        num = num + wgt * acc_ref[br]
        den = den + wgt * l_ref[br]
    out_ref[0] = (num / den).astype(out_ref.dtype)


def _attention(q, k, v):
    bsz, n_pairs, s, _ = q.shape
    tq = ATTN_Q_TILE
    assert s % tq == 0 and tq == ATTN_BLOCK_Q * max(DILATIONS) and s >= 2 * tq
    assert len(DILATIONS) == 3 and DILATIONS[0] == 1 and s % (max(DILATIONS) * ATTN_RELAYOUT_ROWS) == 0
    nb = len(DILATIONS)
    q_spec = pl.BlockSpec((1, 1, tq, LANES), lambda b, hp, i: (b, hp, i, 0))
    kv_spec = pl.BlockSpec((1, 1, s, LANES), lambda b, hp, i: (b, hp, 0, 0))
    return pl.pallas_call(
        _attn_kernel,
        grid=(bsz, n_pairs, s // tq),
        in_specs=[q_spec, kv_spec, kv_spec],
        out_specs=pl.BlockSpec((1, tq, LANES), lambda b, hp, i: (b, i, hp)),
        out_shape=jax.ShapeDtypeStruct((bsz, s, n_pairs * LANES), BF16),
        scratch_shapes=[
            pltpu.VMEM((nb, s, LANES), BF16),
            pltpu.VMEM((nb, s, LANES), BF16),
            pltpu.VMEM((s, LANES), F32),
            pltpu.VMEM((2, ATTN_BLOCK_Q, ATTN_BLOCK_Q + ATTN_WINDOW_STEPS), F32),
        ] + [pltpu.VMEM((nb, tq, LANES), F32)] * 3,
        compiler_params=pltpu.CompilerParams(
            dimension_semantics=("arbitrary", "arbitrary", "arbitrary"), vmem_limit_bytes=VMEM_LIMIT_BYTES),
        name="dilated_attention",
    )(q, k, v)


def kernel(x, p, mix_norm, w_in_ab, pool_w, pool_scale, hgrn_lb, hgrn_o_norm, w_out_ab, w_qkv, q_norm, k_norm,
           w_o, mlp_norm, w_up, w_down, ple_norm, w_ple, w_ple_gate):
    depth = p.shape[0]
    s = x.shape[1]
    h = x
    cos = sin = None
    for layer in range(depth):
        if layer % 2 == 0:
            e = layer // 2
            mix = _mixer0(h, mix_norm[layer], w_in_ab[e], pool_w[e], pool_scale[e], hgrn_lb, hgrn_o_norm[e], layer)
            w_mix = w_out_ab[e]
        else:
            ci = layer // 2
            if cos is None:
                cos, sin = _rope_tables(s)
            q, k, v = _qkv(h, mix_norm[layer], w_qkv[ci], q_norm[ci], k_norm[ci], cos, sin)
            mix = _attention(q, k, v)
            w_mix = w_o[ci]
        h = _post(h, mix, p[layer], w_mix, mlp_norm[layer], w_up[layer], w_down[layer], ple_norm[layer],
                  w_ple_gate[layer], w_ple[layer])
    return h
```

```python
import functools

import jax
import jax.numpy as jnp
from jax import lax
from jax.experimental import pallas as pl
from jax.experimental.pallas import tpu as pltpu

F32 = jnp.float32
BF16 = jnp.bfloat16

NORM_EPS = 1e-6
POOL_WINDOWS = (2, 4, 8, 16)
POOL_GROUP_DIM = 128
POOL_CARRY_ROWS = 16
HGRN_HEAD_DIM = 128
HGRN_CHUNK = 64
HGRN_DIAG = 8
HGRN_GROUP = 8
HGRN_BOUNDED_LOG2_RANGE = 120.0
ATTN_HEAD_DIM = 64
ATTN_WINDOW_STEPS = 128
DILATIONS = (1, 4, 16)
ATTN_BLOCK_Q = 128
ROPE_THETA = 10000.0
LANES = 128
MASK_VALUE = -1e30
LOG2_E = 1.4426950408889634

VMEM_LIMIT_BYTES = 56 * 1024 * 1024

TOKEN_TILE = 512
ATTN_Q_TILE = 2048


def _dot(a, b):
    return jnp.dot(a, b, preferred_element_type=F32)


def _dot_nt(a, b):
    return lax.dot_general(a, b, (((1,), (1,)), ((), ())), preferred_element_type=F32)


def _dot_tn(a, b):
    return lax.dot_general(a, b, (((0,), (0,)), ((), ())), preferred_element_type=F32)


def _rms(x, g):
    ms = jnp.mean(x * x, axis=-1, keepdims=True)
    return x * lax.rsqrt(ms + NORM_EPS) * g


def _sigmoid(x):
    return 1.0 / (1.0 + jnp.exp(-x))


def _const_spec(shape):
    nd = len(shape)
    return pl.BlockSpec(shape, lambda *_: (0,) * nd, pipeline_mode=pl.Buffered(1))


def _hgrn_diag_scores(qb, bb, b_row, k_row, lane_mod):
    n = qb.shape[0]
    scores = jnp.zeros((n, HGRN_CHUNK), F32)
    for s in range(n):
        e = jnp.exp2(bb - b_row(s))
        a = jnp.sum(qb * e * k_row(s), axis=-1, keepdims=True)
        scores = jnp.where(lane_mod == s, a, scores)
    return scores


def _hgrn_level_scores(q, kk, b, span):
    c = q.shape[0]
    zeros = jnp.zeros((span, q.shape[1]), F32)
    q_parts, k_parts = [], []
    for start in range(0, c, 2 * span):
        mid = start + span
        anchor = b[mid - 1:mid, :]
        lo = slice(start, mid)
        hi = slice(mid, mid + span)
        q_parts += [zeros, q[hi] * jnp.exp2(b[hi] - anchor)]
        k_parts += [kk[lo] * jnp.exp2(anchor - b[lo]), zeros]
    qe = jnp.concatenate(q_parts, axis=0).astype(BF16)
    ke = jnp.concatenate(k_parts, axis=0).astype(BF16)
    return _dot_nt(qe, ke)


def _hgrn_chunk_head(q, kk, v, b, st, masks, b_row, k_row):
    c = HGRN_CHUNK
    d = HGRN_DIAG
    causal_diag, same_parent, lane_mod = masks
    o_inter = _dot_nt((q * jnp.exp2(b)).astype(BF16), st.astype(BF16))
    diag = jnp.concatenate(
        [_hgrn_diag_scores(q[i:i + d], b[i:i + d], lambda s, i=i: b_row(i + s), lambda s, i=i: k_row(i + s),
                           lane_mod) for i in range(0, c, d)], axis=0)
    span = c // 2
    scores = _hgrn_level_scores(q, kk, b, span)
    while span > d:
        span //= 2
        scores = jnp.where(same_parent[2 * span], _hgrn_level_scores(q, kk, b, span), scores)
    scores = jnp.where(same_parent[d], jnp.where(causal_diag, diag, 0.0), scores)
    o = o_inter + _dot(scores.astype(BF16), v.astype(BF16))
    b_last = b[c - 1:c, :]
    ke = (kk * jnp.exp2(b_last - b)).astype(BF16)
    st_new = jnp.exp2(b_last) * st + _dot_tn(v.astype(BF16), ke)
    return o, st_new


def _hgrn_chunk_head_bounded(q, kk, v, b, st, causal):
    c = HGRN_CHUNK
    qe = (q * jnp.exp2(b)).astype(BF16)
    ke = (kk * jnp.exp2(-b)).astype(BF16)
    scores = jnp.where(causal, _dot_nt(qe, ke), 0.0)
    o = _dot_nt(qe, st.astype(BF16)) + _dot(scores.astype(BF16), v.astype(BF16))
    b_last = b[c - 1:c, :]
    ks = (kk * jnp.exp2(b_last - b)).astype(BF16)
    st_new = jnp.exp2(b_last) * st + _dot_tn(v.astype(BF16), ks)
    return o, st_new


def _hgrn_masks():
    c = HGRN_CHUNK
    r = lax.broadcasted_iota(jnp.int32, (c, c), 0)
    s = lax.broadcasted_iota(jnp.int32, (c, c), 1)
    same_parent = {}
    n = HGRN_DIAG
    while n < c:
        shift = n.bit_length() - 1
        same_parent[n] = lax.shift_right_logical(r, shift) == lax.shift_right_logical(s, shift)
        n *= 2
    lane_mod = lax.broadcasted_iota(jnp.int32, (HGRN_DIAG, c), 1) & (HGRN_DIAG - 1)
    return r >= s, same_parent, lane_mod


def _mixer0_kernel(h_ref, g_ref, win_ref, poolw_ref, pscale_ref, lb_ref, onorm_ref, out_ref,
                   z_ref, upad_ref, state_ref, brow_ref, krow_ref, *, layer):
    t = h_ref.shape[1]
    si = pl.program_id(1)
    pool_w = len(POOL_WINDOWS) * POOL_GROUP_DIM
    n_heads = state_ref.shape[0]
    hw = n_heads * HGRN_HEAD_DIM

    @pl.when(si == 0)
    def _():
        state_ref[...] = jnp.zeros_like(state_ref)
        upad_ref[0:POOL_CARRY_ROWS, :] = jnp.zeros((POOL_CARRY_ROWS, pool_w), F32)

    hn = _rms(h_ref[0], g_ref[...]).astype(BF16)
    z_ref[...] = _dot(hn, win_ref[...])

    upad_ref[POOL_CARRY_ROWS:POOL_CARRY_ROWS + t, :] = z_ref[:, 0:pool_w]
    pos = lax.broadcasted_iota(jnp.int32, (t, 1), 0) + si * t
    for gi, w in enumerate(POOL_WINDOWS):
        cols = slice(gi * POOL_GROUP_DIM, (gi + 1) * POOL_GROUP_DIM)
        u = upad_ref[POOL_CARRY_ROWS:POOL_CARRY_ROWS + t, cols]
        acc = u
        for j in range(1, w):
            acc = acc + upad_ref[POOL_CARRY_ROWS - j:POOL_CARRY_ROWS - j + t, cols]
        cnt = jnp.minimum(pos + 1, w).astype(F32)
        dlt = acc / cnt - u
        y = _dot(dlt.astype(BF16), poolw_ref[gi]) * pscale_ref[:, cols]
        out_ref[0, :, cols] = y.astype(out_ref.dtype)
    upad_ref[0:POOL_CARRY_ROWS, :] = upad_ref[t:t + POOL_CARRY_ROWS, :]

    lbraw = lb_ref[...]
    lbe = jnp.exp(lbraw - jnp.max(lbraw, axis=0, keepdims=True))
    lbs = lbe / jnp.sum(lbe, axis=0, keepdims=True)
    lb = jnp.sum(lbs[0:layer + 1], axis=0, keepdims=True)
    c = HGRN_CHUNK
    ri = lax.broadcasted_iota(jnp.int32, (c, c), 0)
    ci = lax.broadcasted_iota(jnp.int32, (c, c), 1)
    ltri = jnp.where(ri >= ci, 1.0, 0.0).astype(BF16)
    onorm = onorm_ref[...]
    masks = _hgrn_masks()

    group = HGRN_GROUP
    head_cols = [slice(hd * HGRN_HEAD_DIM, (hd + 1) * HGRN_HEAD_DIM) for hd in range(n_heads)]

    def group_body(gi, carry):
        base = pl.multiple_of(gi * (group * c), group * c)
        zq, zi, zg, kk, b = [], [], [], [], []
        for u in range(group):
            rows = pl.ds(base + u * c, c)
            zq.append(z_ref[rows, pool_w:pool_w + hw])
            zf = z_ref[rows, pool_w + hw:pool_w + 2 * hw]
            zi.append(z_ref[rows, pool_w + 2 * hw:pool_w + 3 * hw])
            zg.append(z_ref[rows, pool_w + 3 * hw:pool_w + 4 * hw])
            f = lb + (1.0 - lb) * _sigmoid(zf)
            logf = jnp.log2(f)
            kk.append(1.0 - f)
            g_hi = logf.astype(BF16)
            r1 = logf - g_hi.astype(F32)
            g_mid = r1.astype(BF16)
            g_lo = (r1 - g_mid.astype(F32)).astype(BF16)
            b.append(_dot(ltri, g_hi) + _dot(ltri, g_mid) + _dot(ltri, g_lo))
        b_min = b[0][c - 1:c, :]
        for u in range(1, group):
            b_min = jnp.minimum(b_min, b[u][c - 1:c, :])
        bounded = jnp.min(b_min) >= -HGRN_BOUNDED_LOG2_RANGE

        def run(chunk_head):
            states = [state_ref[hd] for hd in range(n_heads)]
            for u in range(group):
                for hd, cols in enumerate(head_cols):
                    o, states[hd] = chunk_head(u, cols, states[hd])
                    o = o * lax.rsqrt(jnp.mean(o * o, axis=-1, keepdims=True) + NORM_EPS) * onorm
                    gate = zg[u][:, cols]
                    o = o * (gate * _sigmoid(gate))
                    out_ref[0, pl.ds(base + u * c, c), pool_w + cols.start:pool_w + cols.stop] = (
                        o.astype(out_ref.dtype))
            for hd in range(n_heads):
                state_ref[hd] = states[hd]

        @pl.when(bounded)
        def _():
            run(lambda u, cols, st: _hgrn_chunk_head_bounded(
                zq[u][:, cols], kk[u][:, cols], zi[u][:, cols], b[u][:, cols], st, masks[0]))

        @pl.when(jnp.logical_not(bounded))
        def _():
            for u in range(group):
                brow_ref[u] = b[u]
                krow_ref[u] = kk[u]
            run(lambda u, cols, st: _hgrn_chunk_head(
                zq[u][:, cols], kk[u][:, cols], zi[u][:, cols], b[u][:, cols], st, masks,
                lambda i: brow_ref[u, i:i + 1, cols], lambda i: krow_ref[u, i:i + 1, cols]))

        return carry

    lax.fori_loop(0, t // (group * c), group_body, 0)


def _mixer0(h, g, w_in, pool_w, pool_scale, hgrn_lb, o_norm, layer):
    bsz, s, d = h.shape
    t = min(TOKEN_TILE, s)
    assert s % t == 0 and t % HGRN_CHUNK == 0
    zw = w_in.shape[1]
    pw = len(POOL_WINDOWS) * POOL_GROUP_DIM
    hw = hgrn_lb.shape[1]
    assert zw == pw + 4 * hw and hw % HGRN_HEAD_DIM == 0
    n_heads = hw // HGRN_HEAD_DIM
    return pl.pallas_call(
        functools.partial(_mixer0_kernel, layer=layer),
        grid=(bsz, s // t),
        in_specs=[
            pl.BlockSpec((1, t, d), lambda b, i: (b, i, 0)),
            _const_spec((1, d)),
            _const_spec((d, zw)),
            _const_spec(pool_w.shape),
            _const_spec((1, pw)),
            _const_spec(hgrn_lb.shape),
            _const_spec((1, HGRN_HEAD_DIM)),
        ],
        out_specs=pl.BlockSpec((1, t, pw + hw), lambda b, i: (b, i, 0)),
        out_shape=jax.ShapeDtypeStruct((bsz, s, pw + hw), BF16),
        scratch_shapes=[
            pltpu.VMEM((t, zw), F32),
            pltpu.VMEM((t + POOL_CARRY_ROWS, pw), F32),
            pltpu.VMEM((n_heads, HGRN_HEAD_DIM, HGRN_HEAD_DIM), F32),
            pltpu.VMEM((HGRN_GROUP, HGRN_CHUNK, hw), F32),
            pltpu.VMEM((HGRN_GROUP, HGRN_CHUNK, hw), F32),
        ],
        compiler_params=pltpu.CompilerParams(
            dimension_semantics=("arbitrary", "arbitrary"), vmem_limit_bytes=VMEM_LIMIT_BYTES),
        name="mixer0",
    )(h, g.reshape(1, d), w_in.astype(BF16), pool_w.astype(BF16), pool_scale.reshape(1, pw),
      hgrn_lb, o_norm.reshape(1, HGRN_HEAD_DIM))


def _post_kernel(h_ref, mix_ref, p_ref, wmix_ref, gm_ref, wup_ref, wdn_ref, gp_ref, wg_ref, wple_ref, out_ref,
                 *, ff_chunk):
    h1 = h_ref[0] + _dot(mix_ref[0], wmix_ref[...])
    hm = _rms(h1, gm_ref[...]).astype(BF16)
    d_ff = wup_ref.shape[1]
    acc = jnp.zeros_like(h1)
    for c0 in range(0, d_ff, ff_chunk):
        up = _dot(hm, wup_ref[:, c0:c0 + ff_chunk])
        act = jnp.square(jnp.maximum(up, 0.0)).astype(BF16)
        acc = acc + _dot(act, wdn_ref[c0:c0 + ff_chunk, :])
    h2 = h1 + acc
    gate = _sigmoid(_dot(_rms(h2, gp_ref[...]).astype(BF16), wg_ref[...]))
    pe = _dot(p_ref[0, 0].astype(BF16), wple_ref[...])
    out_ref[0] = h2 + pe * gate


def _post(h, mix, p, layer, w_mix, g_mlp, w_up, w_down, g_ple, w_gate, w_ple):
    bsz, s, d = h.shape
    t = min(TOKEN_TILE, s)
    assert s % t == 0
    d_mix = mix.shape[-1]
    d_ff = w_up.shape[1]
    d_p = p.shape[-1]
    ff_chunk = min(1024, d_ff)
    assert d_ff % ff_chunk == 0
    tok = lambda width: pl.BlockSpec((1, t, width), lambda b, i: (b, i, 0))
    return pl.pallas_call(
        functools.partial(_post_kernel, ff_chunk=ff_chunk),
        grid=(bsz, s // t),
        in_specs=[
            tok(d), tok(d_mix),
            pl.BlockSpec((1, 1, t, d_p), lambda b, i: (layer, b, i, 0)),
            _const_spec((d_mix, d)),
            _const_spec((1, d)),
            _const_spec((d, d_ff)),
            _const_spec((d_ff, d)),
            _const_spec((1, d)),
            _const_spec((d, d)),
            _const_spec((d_p, d)),
        ],
        out_specs=tok(d),
        out_shape=jax.ShapeDtypeStruct((bsz, s, d), F32),
        compiler_params=pltpu.CompilerParams(
            dimension_semantics=("arbitrary", "arbitrary"), vmem_limit_bytes=VMEM_LIMIT_BYTES),
        name="post",
    )(h, mix, p, w_mix.astype(BF16), g_mlp.reshape(1, d), w_up.astype(BF16), w_down.astype(BF16),
      g_ple.reshape(1, d), w_gate.astype(BF16), w_ple.astype(BF16))


def _rope_kernel(inv_ref, cos_ref, sin_ref):
    t = cos_ref.shape[0]
    pos = (lax.broadcasted_iota(jnp.int32, (t, LANES), 0) + pl.program_id(0) * t).astype(F32)
    lane = lax.broadcasted_iota(jnp.int32, (t, LANES), 1)
    ang = pos * inv_ref[...]
    cos_ref[...] = jnp.cos(ang)
    sin = jnp.sin(ang)
    sin_ref[...] = jnp.where(lane < LANES // 2, -sin, sin)


def _rope_tables(s):
    half = ATTN_HEAD_DIM // 2
    inv = ROPE_THETA ** (-jnp.arange(half, dtype=F32) / half)
    inv = jnp.tile(inv, LANES // half).reshape(1, LANES)
    t = min(1024, s)
    assert s % t == 0
    return pl.pallas_call(
        _rope_kernel,
        grid=(s // t,),
        in_specs=[pl.BlockSpec((1, LANES), lambda i: (0, 0))],
        out_specs=[pl.BlockSpec((t, LANES), lambda i: (i, 0))] * 2,
        out_shape=[jax.ShapeDtypeStruct((s, LANES), F32)] * 2,
        name="rope_tables",
    )(inv)


def _rope_lane_perm(d):
    half = ATTN_HEAD_DIM // 2
    lane = jnp.arange(LANES)
    col_in_pair = ((lane // half) % 2) * ATTN_HEAD_DIM + (lane // (2 * half)) * half + lane % half
    return (jnp.arange(d // LANES)[:, None] * LANES + col_in_pair[None, :]).reshape(-1)


def _qkv_kernel(h_ref, g_ref, w_ref, qg_ref, kg_ref, cos_ref, sin_ref, same_head_ref, *refs):
    out_refs, (stage_ref, tmp_ref) = refs[:-2], refs[-2:]
    t, d = h_ref.shape[1], h_ref.shape[2]
    d1, d2 = DILATIONS[1], DILATIONS[2]
    n1, n2 = t // d1, t // d2
    group = same_head_ref.shape[0]
    hn = _rms(h_ref[0], g_ref[...]).astype(BF16)
    cos = cos_ref[...]
    sin = sin_ref[...]
    emitted = [0]

    def emit(xb, outs, hp):
        nat_ref, l1_ref, l2_ref = outs
        slot = emitted[0] % stage_ref.shape[0]
        emitted[0] += 1
        nat_ref[0, hp] = xb.astype(BF16)
        stage_ref[slot] = xb
        for r in range(d1):
            y = stage_ref[slot, pl.ds(r, n1, stride=d1), :]
            l1_ref[0, hp, r] = y.astype(BF16)
            tmp_ref[slot, r * n1:(r + 1) * n1, :] = y
        for r in range(d2):
            y = tmp_ref[slot, pl.ds((r % d1) * n1 + r // d1, n2, stride=d2 // d1), :]
            l2_ref[0, hp, r] = y.astype(BF16)

    z = _dot(hn, w_ref[...])
    for c0 in range(0, 3 * d, group):
        zc = z[:, c0:c0 + group]
        which, col = divmod(c0, d)
        outs = out_refs[3 * which:3 * which + 3]
        if which < 2:
            ss = _dot((zc * zc).astype(BF16), same_head_ref[...])
            gain = (qg_ref, kg_ref)[which][:, col:col + group]
            zc = zc * lax.rsqrt(ss * (1.0 / ATTN_HEAD_DIM) + NORM_EPS) * gain
        for j in range(group // LANES):
            xb = zc[:, j * LANES:(j + 1) * LANES]
            if which < 2:
                xb = xb * cos + pltpu.roll(xb, LANES // 2, 1) * sin
            emit(xb, outs, (col + j * LANES) // LANES)


def _qkv(h, g, w_qkv, q_gain, k_gain, cos, sin):
    bsz, s, d = h.shape
    t = min(TOKEN_TILE, s)
    assert s % t == 0 and w_qkv.shape[1] == 3 * d and d % LANES == 0
    n_pairs = d // LANES
    d1, d2 = DILATIONS[1], DILATIONS[2]
    assert t % (d2 * 16) == 0
    perm = _rope_lane_perm(d)
    w = jnp.concatenate([w_qkv[:, :d][:, perm], w_qkv[:, d:2 * d][:, perm], w_qkv[:, 2 * d:]], axis=1)
    q_gain_lanes = (q_gain * (ATTN_HEAD_DIM ** -0.5 * LOG2_E))[perm % ATTN_HEAD_DIM]
    k_gain_lanes = k_gain[perm % ATTN_HEAD_DIM]
    group = 2 * LANES
    head_of_lane = perm[:group] // ATTN_HEAD_DIM
    same_head = (head_of_lane[:, None] == head_of_lane[None, :]).astype(BF16)
    shapes = [(bsz, n_pairs, s, LANES), (bsz, n_pairs, d1, s // d1, LANES), (bsz, n_pairs, d2, s // d2, LANES)]
    specs = [
        pl.BlockSpec((1, n_pairs, t, LANES), lambda b, i: (b, 0, i, 0)),
        pl.BlockSpec((1, n_pairs, d1, t // d1, LANES), lambda b, i: (b, 0, 0, i, 0)),
        pl.BlockSpec((1, n_pairs, d2, t // d2, LANES), lambda b, i: (b, 0, 0, i, 0)),
    ]
    outs = pl.pallas_call(
        _qkv_kernel,
        grid=(bsz, s // t),
        in_specs=[
            pl.BlockSpec((1, t, d), lambda b, i: (b, i, 0)),
            _const_spec((1, d)),
            _const_spec((d, 3 * d)),
            _const_spec((1, d)),
            _const_spec((1, d)),
            pl.BlockSpec((t, LANES), lambda b, i: (i, 0)),
            pl.BlockSpec((t, LANES), lambda b, i: (i, 0)),
            _const_spec((group, group)),
        ],
        out_specs=specs * 3,
        out_shape=[jax.ShapeDtypeStruct(sh, BF16) for sh in shapes] * 3,
        scratch_shapes=[pltpu.VMEM((2, t, LANES), F32), pltpu.VMEM((2, t, LANES), F32)],
        compiler_params=pltpu.CompilerParams(
            dimension_semantics=("arbitrary", "arbitrary"), vmem_limit_bytes=VMEM_LIMIT_BYTES),
        name="qkv",
    )(h, g.reshape(1, d), w.astype(BF16), q_gain_lanes.reshape(1, d), k_gain_lanes.reshape(1, d), cos, sin,
      same_head)
    return outs[0:3], outs[3:6], outs[6:9]


def _attn_kernel(q1_ref, q4_ref, q16_ref, k1_ref, k4_ref, k16_ref, v1_ref, v4_ref, v16_ref, out_ref,
                 bias_ref, acc_ref, m_ref, l_ref):
    tq = q1_ref.shape[2]
    bq = ATTN_BLOCK_Q
    w = ATTN_WINDOW_STEPS
    nk = bq + w
    t0 = pl.program_id(2) * tq

    def rows_of(ref, res, start, size):
        if len(ref.shape) == 4:
            return ref[0, 0, pl.ds(start, size), :]
        return ref[0, 0, res, pl.ds(start, size), :]

    i_minus_j = (lax.broadcasted_iota(jnp.int32, (bq, nk), 0) - lax.broadcasted_iota(jnp.int32, (bq, nk), 1))
    for variant, koff in enumerate((w, 0)):
        dist = i_minus_j + koff
        bias_ref[variant] = jnp.where((dist >= 0) & (dist <= w), 0.0, MASK_VALUE)

    lane = lax.broadcasted_iota(jnp.int32, (bq, LANES), 1)
    head0 = lane < ATTN_HEAD_DIM
    q_head0 = jnp.where((lane // (ATTN_HEAD_DIM // 2)) % 2 == 0, 1.0, 0.0).astype(BF16)
    q_head1 = (1.0 - q_head0.astype(F32)).astype(BF16)
    ones = jnp.ones((nk, LANES), BF16)

    for br, (dil, q_ref, k_ref, v_ref) in enumerate(
            zip(DILATIONS, (q1_ref, q4_ref, q16_ref), (k1_ref, k4_ref, k16_ref), (v1_ref, v4_ref, v16_ref))):
        blocks_per_res = tq // (dil * bq)
        for res in range(dil):
            for blk in range(blocks_per_res):
                l_k0 = t0 // dil + bq * blk - w
                variant = (l_k0 < 0).astype(jnp.int32)
                row0 = pl.multiple_of(jnp.maximum(l_k0, 0), bq)
                qb = rows_of(q_ref, res, blk * bq, bq)
                kb = rows_of(k_ref, res, row0, nk)
                vb = rows_of(v_ref, res, row0, nk)
                qq = jnp.concatenate([qb * q_head0, qb * q_head1], axis=0)
                s = _dot_nt(qq, kb)
                bias = bias_ref[variant]
                s = jnp.concatenate([s[:bq] + bias, s[bq:] + bias], axis=0)
                m = jnp.max(s, axis=-1, keepdims=True)
                p = jnp.exp2(s - m).astype(BF16)
                pv = _dot(p, jnp.concatenate([vb, ones], axis=1))
                rows = pl.ds(res + dil * bq * blk, bq, stride=dil)
                acc_ref[br, rows, :] = jnp.where(head0, pv[:bq, :LANES], pv[bq:, :LANES])
                l_ref[br, rows, :] = jnp.where(head0, pv[:bq, LANES:], pv[bq:, LANES:])
                m_ref[br, rows, :] = jnp.where(head0, m[:bq], m[bq:])

    m_all = jnp.max(m_ref[...], axis=0)
    num = jnp.zeros((tq, LANES), F32)
    den = jnp.zeros((tq, LANES), F32)
    for br in range(len(DILATIONS)):
        wgt = jnp.exp2(m_ref[br] - m_all)
        num = num + wgt * acc_ref[br]
        den = den + wgt * l_ref[br]
    out_ref[0] = (num / den).astype(out_ref.dtype)


def _attention(q, k, v):
    bsz, n_pairs, s, _ = q[0].shape
    tq = ATTN_Q_TILE
    assert s % tq == 0 and tq == ATTN_BLOCK_Q * max(DILATIONS) and s >= 2 * tq
    assert len(DILATIONS) == 3 and DILATIONS[0] == 1
    nb = len(DILATIONS)

    def spec(dil, rows, whole):
        if dil == 1:
            return pl.BlockSpec((1, 1, rows, LANES), lambda b, hp, i: (b, hp, 0 if whole else i, 0))
        return pl.BlockSpec((1, 1, dil, rows, LANES), lambda b, hp, i: (b, hp, 0, 0 if whole else i, 0))

    q_specs = [spec(dil, tq // dil, False) for dil in DILATIONS]
    kv_specs = [spec(dil, s // dil, True) for dil in DILATIONS]
    return pl.pallas_call(
        _attn_kernel,
        grid=(bsz, n_pairs, s // tq),
        in_specs=q_specs + kv_specs + kv_specs,
        out_specs=pl.BlockSpec((1, tq, LANES), lambda b, hp, i: (b, i, hp)),
        out_shape=jax.ShapeDtypeStruct((bsz, s, n_pairs * LANES), BF16),
        scratch_shapes=[
            pltpu.VMEM((2, ATTN_BLOCK_Q, ATTN_BLOCK_Q + ATTN_WINDOW_STEPS), F32),
        ] + [pltpu.VMEM((nb, tq, LANES), F32)] * 3,
        compiler_params=pltpu.CompilerParams(
            dimension_semantics=("arbitrary", "arbitrary", "arbitrary"), vmem_limit_bytes=VMEM_LIMIT_BYTES),
        name="dilated_attention",
    )(*q, *k, *v)


def kernel(x, p, mix_norm, w_in_ab, pool_w, pool_scale, hgrn_lb, hgrn_o_norm, w_out_ab, w_qkv, q_norm, k_norm,
           w_o, mlp_norm, w_up, w_down, ple_norm, w_ple, w_ple_gate):
    depth = p.shape[0]
    s = x.shape[1]
    h = x
    cos = sin = None
    for layer in range(depth):
        if layer % 2 == 0:
            e = layer // 2
            mix = _mixer0(h, mix_norm[layer], w_in_ab[e], pool_w[e], pool_scale[e], hgrn_lb, hgrn_o_norm[e], layer)
            w_mix = w_out_ab[e]
        else:
            ci = layer // 2
            if cos is None:
                cos, sin = _rope_tables(s)
            q, k, v = _qkv(h, mix_norm[layer], w_qkv[ci], q_norm[ci], k_norm[ci], cos, sin)
            mix = _attention(q, k, v)
            w_mix = w_o[ci]
        h = _post(h, mix, p, layer, w_mix, mlp_norm[layer], w_up[layer], w_down[layer], ple_norm[layer],
                  w_ple_gate[layer], w_ple[layer])
    return h
```

```python
import functools

import jax
import jax.numpy as jnp
from jax import lax
from jax.experimental import pallas as pl
from jax.experimental.pallas import tpu as pltpu

F32 = jnp.float32
BF16 = jnp.bfloat16

NORM_EPS = 1e-6
POOL_WINDOWS = (2, 4, 8, 16)
POOL_GROUP_DIM = 128
POOL_CARRY_ROWS = 16
HGRN_HEAD_DIM = 128
HGRN_CHUNK = 64
HGRN_DIAG = 8
HGRN_GROUP = 8
HGRN_BOUNDED_LOG2_RANGE = 120.0
ATTN_HEAD_DIM = 64
ATTN_WINDOW_STEPS = 128
DILATIONS = (1, 4, 16)
ATTN_BLOCK_Q = 128
ROPE_THETA = 10000.0
LANES = 128
MASK_VALUE = -1e30
LOG2_E = 1.4426950408889634

VMEM_LIMIT_BYTES = 56 * 1024 * 1024

TOKEN_TILE = 512
ATTN_Q_TILE = 2048


def _dot(a, b):
    return jnp.dot(a, b, preferred_element_type=F32)


def _dot_nt(a, b):
    return lax.dot_general(a, b, (((1,), (1,)), ((), ())), preferred_element_type=F32)


def _dot_tn(a, b):
    return lax.dot_general(a, b, (((0,), (0,)), ((), ())), preferred_element_type=F32)


def _rms(x, g):
    ms = jnp.mean(x * x, axis=-1, keepdims=True)
    return x * lax.rsqrt(ms + NORM_EPS) * g


def _sigmoid(x):
    return 1.0 / (1.0 + jnp.exp(-x))


def _const_spec(shape):
    nd = len(shape)
    return pl.BlockSpec(shape, lambda *_: (0,) * nd, pipeline_mode=pl.Buffered(1))


def _hgrn_diag_scores(qb, bb, b_row, k_row, lane_mod):
    n = qb.shape[0]
    scores = jnp.zeros((n, HGRN_CHUNK), F32)
    for s in range(n):
        e = jnp.exp2(bb - b_row(s))
        a = jnp.sum(qb * e * k_row(s), axis=-1, keepdims=True)
        scores = jnp.where(lane_mod == s, a, scores)
    return scores


def _hgrn_level_scores(q, kk, b, span):
    c = q.shape[0]
    zeros = jnp.zeros((span, q.shape[1]), F32)
    q_parts, k_parts = [], []
    for start in range(0, c, 2 * span):
        mid = start + span
        anchor = b[mid - 1:mid, :]
        lo = slice(start, mid)
        hi = slice(mid, mid + span)
        q_parts += [zeros, q[hi] * jnp.exp2(b[hi] - anchor)]
        k_parts += [kk[lo] * jnp.exp2(anchor - b[lo]), zeros]
    qe = jnp.concatenate(q_parts, axis=0).astype(BF16)
    ke = jnp.concatenate(k_parts, axis=0).astype(BF16)
    return _dot_nt(qe, ke)


def _hgrn_chunk_head(q, kk, v, b, st, masks, b_row, k_row):
    c = HGRN_CHUNK
    d = HGRN_DIAG
    causal_diag, same_parent, lane_mod = masks
    o_inter = _dot_nt((q * jnp.exp2(b)).astype(BF16), st.astype(BF16))
    diag = jnp.concatenate(
        [_hgrn_diag_scores(q[i:i + d], b[i:i + d], lambda s, i=i: b_row(i + s), lambda s, i=i: k_row(i + s),
                           lane_mod) for i in range(0, c, d)], axis=0)
    span = c // 2
    scores = _hgrn_level_scores(q, kk, b, span)
    while span > d:
        span //= 2
        scores = jnp.where(same_parent[2 * span], _hgrn_level_scores(q, kk, b, span), scores)
    scores = jnp.where(same_parent[d], jnp.where(causal_diag, diag, 0.0), scores)
    o = o_inter + _dot(scores.astype(BF16), v.astype(BF16))
    b_last = b[c - 1:c, :]
    ke = (kk * jnp.exp2(b_last - b)).astype(BF16)
    st_new = jnp.exp2(b_last) * st + _dot_tn(v.astype(BF16), ke)
    return o, st_new


def _hgrn_chunk_head_bounded(q, kk, v, b, st, causal):
    c = HGRN_CHUNK
    qe = (q * jnp.exp2(b)).astype(BF16)
    ke = (kk * jnp.exp2(-b)).astype(BF16)
    scores = jnp.where(causal, _dot_nt(qe, ke), 0.0)
    o = _dot_nt(qe, st.astype(BF16)) + _dot(scores.astype(BF16), v.astype(BF16))
    b_last = b[c - 1:c, :]
    ks = (kk * jnp.exp2(b_last - b)).astype(BF16)
    st_new = jnp.exp2(b_last) * st + _dot_tn(v.astype(BF16), ks)
    return o, st_new


def _hgrn_masks():
    c = HGRN_CHUNK
    r = lax.broadcasted_iota(jnp.int32, (c, c), 0)
    s = lax.broadcasted_iota(jnp.int32, (c, c), 1)
    same_parent = {}
    n = HGRN_DIAG
    while n < c:
        shift = n.bit_length() - 1
        same_parent[n] = lax.shift_right_logical(r, shift) == lax.shift_right_logical(s, shift)
        n *= 2
    lane_mod = lax.broadcasted_iota(jnp.int32, (HGRN_DIAG, c), 1) & (HGRN_DIAG - 1)
    return r >= s, same_parent, lane_mod


def _mixer0_kernel(h_ref, g_ref, win_ref, poolw_ref, pscale_ref, lb_ref, onorm_ref, out_ref,
                   z_ref, upad_ref, state_ref, brow_ref, krow_ref, *, layer):
    t = h_ref.shape[1]
    si = pl.program_id(1)
    pool_w = len(POOL_WINDOWS) * POOL_GROUP_DIM
    n_heads = state_ref.shape[0]
    hw = n_heads * HGRN_HEAD_DIM

    @pl.when(si == 0)
    def _():
        state_ref[...] = jnp.zeros_like(state_ref)
        upad_ref[0:POOL_CARRY_ROWS, :] = jnp.zeros((POOL_CARRY_ROWS, pool_w), F32)

    hn = _rms(h_ref[0], g_ref[...]).astype(BF16)
    z_ref[...] = _dot(hn, win_ref[...])

    upad_ref[POOL_CARRY_ROWS:POOL_CARRY_ROWS + t, :] = z_ref[:, 0:pool_w]
    pos = lax.broadcasted_iota(jnp.int32, (t, 1), 0) + si * t
    for gi, w in enumerate(POOL_WINDOWS):
        cols = slice(gi * POOL_GROUP_DIM, (gi + 1) * POOL_GROUP_DIM)
        u = upad_ref[POOL_CARRY_ROWS:POOL_CARRY_ROWS + t, cols]
        acc = u
        for j in range(1, w):
            acc = acc + upad_ref[POOL_CARRY_ROWS - j:POOL_CARRY_ROWS - j + t, cols]
        cnt = jnp.minimum(pos + 1, w).astype(F32)
        dlt = acc / cnt - u
        y = _dot(dlt.astype(BF16), poolw_ref[gi]) * pscale_ref[:, cols]
        out_ref[0, :, cols] = y.astype(out_ref.dtype)
    upad_ref[0:POOL_CARRY_ROWS, :] = upad_ref[t:t + POOL_CARRY_ROWS, :]

    lbraw = lb_ref[...]
    lbe = jnp.exp(lbraw - jnp.max(lbraw, axis=0, keepdims=True))
    lbs = lbe / jnp.sum(lbe, axis=0, keepdims=True)
    lb = jnp.sum(lbs[0:layer + 1], axis=0, keepdims=True)
    c = HGRN_CHUNK
    ri = lax.broadcasted_iota(jnp.int32, (c, c), 0)
    ci = lax.broadcasted_iota(jnp.int32, (c, c), 1)
    ltri = jnp.where(ri >= ci, 1.0, 0.0).astype(BF16)
    onorm = onorm_ref[...]
    masks = _hgrn_masks()

    group = t // c
    head_cols = [slice(hd * HGRN_HEAD_DIM, (hd + 1) * HGRN_HEAD_DIM) for hd in range(n_heads)]

    zq, zi, zg, kk, b = [], [], [], [], []
    for u in range(group):
        rows = slice(u * c, (u + 1) * c)
        zq.append(z_ref[rows, pool_w:pool_w + hw])
        zf = z_ref[rows, pool_w + hw:pool_w + 2 * hw]
        zi.append(z_ref[rows, pool_w + 2 * hw:pool_w + 3 * hw])
        zg.append(z_ref[rows, pool_w + 3 * hw:pool_w + 4 * hw])
        f = lb + (1.0 - lb) * _sigmoid(zf)
        logf = jnp.log2(f)
        kk.append(1.0 - f)
        g_hi = logf.astype(BF16)
        r1 = logf - g_hi.astype(F32)
        g_mid = r1.astype(BF16)
        g_lo = (r1 - g_mid.astype(F32)).astype(BF16)
        b.append(_dot(ltri, g_hi) + _dot(ltri, g_mid) + _dot(ltri, g_lo))
    b_min = b[0][c - 1:c, :]
    for u in range(1, group):
        b_min = jnp.minimum(b_min, b[u][c - 1:c, :])
    bounded = jnp.min(b_min) >= -HGRN_BOUNDED_LOG2_RANGE

    def run(chunk_head):
        states = [state_ref[hd] for hd in range(n_heads)]
        for u in range(group):
            for hd, cols in enumerate(head_cols):
                o, states[hd] = chunk_head(u, cols, states[hd])
                o = o * lax.rsqrt(jnp.mean(o * o, axis=-1, keepdims=True) + NORM_EPS) * onorm
                gate = zg[u][:, cols]
                o = o * (gate * _sigmoid(gate))
                out_ref[0, u * c:(u + 1) * c, pool_w + cols.start:pool_w + cols.stop] = o.astype(out_ref.dtype)
        for hd in range(n_heads):
            state_ref[hd] = states[hd]

    @pl.when(bounded)
    def _():
        run(lambda u, cols, st: _hgrn_chunk_head_bounded(
            zq[u][:, cols], kk[u][:, cols], zi[u][:, cols], b[u][:, cols], st, masks[0]))

    @pl.when(jnp.logical_not(bounded))
    def _():
        for u in range(group):
            brow_ref[u] = b[u]
            krow_ref[u] = kk[u]
        run(lambda u, cols, st: _hgrn_chunk_head(
            zq[u][:, cols], kk[u][:, cols], zi[u][:, cols], b[u][:, cols], st, masks,
            lambda i: brow_ref[u, i:i + 1, cols], lambda i: krow_ref[u, i:i + 1, cols]))


def _mixer0(h, g, w_in, pool_w, pool_scale, hgrn_lb, o_norm, layer):
    bsz, s, d = h.shape
    t = min(TOKEN_TILE, s)
    assert s % t == 0 and t % HGRN_CHUNK == 0
    zw = w_in.shape[1]
    pw = len(POOL_WINDOWS) * POOL_GROUP_DIM
    hw = hgrn_lb.shape[1]
    assert zw == pw + 4 * hw and hw % HGRN_HEAD_DIM == 0
    n_heads = hw // HGRN_HEAD_DIM
    return pl.pallas_call(
        functools.partial(_mixer0_kernel, layer=layer),
        grid=(bsz, s // t),
        in_specs=[
            pl.BlockSpec((1, t, d), lambda b, i: (b, i, 0)),
            _const_spec((1, d)),
            _const_spec((d, zw)),
            _const_spec(pool_w.shape),
            _const_spec((1, pw)),
            _const_spec(hgrn_lb.shape),
            _const_spec((1, HGRN_HEAD_DIM)),
        ],
        out_specs=pl.BlockSpec((1, t, pw + hw), lambda b, i: (b, i, 0)),
        out_shape=jax.ShapeDtypeStruct((bsz, s, pw + hw), BF16),
        scratch_shapes=[
            pltpu.VMEM((t, zw), F32),
            pltpu.VMEM((t + POOL_CARRY_ROWS, pw), F32),
            pltpu.VMEM((n_heads, HGRN_HEAD_DIM, HGRN_HEAD_DIM), F32),
            pltpu.VMEM((t // HGRN_CHUNK, HGRN_CHUNK, hw), F32),
            pltpu.VMEM((t // HGRN_CHUNK, HGRN_CHUNK, hw), F32),
        ],
        compiler_params=pltpu.CompilerParams(
            dimension_semantics=("arbitrary", "arbitrary"), vmem_limit_bytes=VMEM_LIMIT_BYTES),
        name="mixer0",
    )(h, g.reshape(1, d), w_in.astype(BF16), pool_w.astype(BF16), pool_scale.reshape(1, pw),
      hgrn_lb, o_norm.reshape(1, HGRN_HEAD_DIM))


def _post_kernel(h_ref, mix_ref, p_ref, wmix_ref, gm_ref, wup_ref, wdn_ref, gp_ref, wg_ref, wple_ref, out_ref,
                 *, ff_chunk):
    h1 = h_ref[0] + _dot(mix_ref[0], wmix_ref[...])
    hm = _rms(h1, gm_ref[...]).astype(BF16)
    d_ff = wup_ref.shape[1]
    acc = jnp.zeros_like(h1)
    for c0 in range(0, d_ff, ff_chunk):
        up = _dot(hm, wup_ref[:, c0:c0 + ff_chunk])
        act = jnp.square(jnp.maximum(up, 0.0)).astype(BF16)
        acc = acc + _dot(act, wdn_ref[c0:c0 + ff_chunk, :])
    h2 = h1 + acc
    gate = _sigmoid(_dot(_rms(h2, gp_ref[...]).astype(BF16), wg_ref[...]))
    pe = _dot(p_ref[0, 0].astype(BF16), wple_ref[...])
    out_ref[0] = h2 + pe * gate


def _post(h, mix, p, layer, w_mix, g_mlp, w_up, w_down, g_ple, w_gate, w_ple):
    bsz, s, d = h.shape
    t = min(TOKEN_TILE, s)
    assert s % t == 0
    d_mix = mix.shape[-1]
    d_ff = w_up.shape[1]
    d_p = p.shape[-1]
    ff_chunk = min(1024, d_ff)
    assert d_ff % ff_chunk == 0
    tok = lambda width: pl.BlockSpec((1, t, width), lambda b, i: (b, i, 0))
    return pl.pallas_call(
        functools.partial(_post_kernel, ff_chunk=ff_chunk),
        grid=(bsz, s // t),
        in_specs=[
            tok(d), tok(d_mix),
            pl.BlockSpec((1, 1, t, d_p), lambda b, i: (layer, b, i, 0)),
            _const_spec((d_mix, d)),
            _const_spec((1, d)),
            _const_spec((d, d_ff)),
            _const_spec((d_ff, d)),
            _const_spec((1, d)),
            _const_spec((d, d)),
            _const_spec((d_p, d)),
        ],
        out_specs=tok(d),
        out_shape=jax.ShapeDtypeStruct((bsz, s, d), F32),
        compiler_params=pltpu.CompilerParams(
            dimension_semantics=("arbitrary", "arbitrary"), vmem_limit_bytes=VMEM_LIMIT_BYTES),
        name="post",
    )(h, mix, p, w_mix.astype(BF16), g_mlp.reshape(1, d), w_up.astype(BF16), w_down.astype(BF16),
      g_ple.reshape(1, d), w_gate.astype(BF16), w_ple.astype(BF16))


def _rope_kernel(inv_ref, cos_ref, sin_ref):
    t = cos_ref.shape[0]
    pos = (lax.broadcasted_iota(jnp.int32, (t, LANES), 0) + pl.program_id(0) * t).astype(F32)
    lane = lax.broadcasted_iota(jnp.int32, (t, LANES), 1)
    ang = pos * inv_ref[...]
    cos_ref[...] = jnp.cos(ang)
    sin = jnp.sin(ang)
    sin_ref[...] = jnp.where(lane < LANES // 2, -sin, sin)


def _rope_tables(s):
    half = ATTN_HEAD_DIM // 2
    inv = ROPE_THETA ** (-jnp.arange(half, dtype=F32) / half)
    inv = jnp.tile(inv, LANES // half).reshape(1, LANES)
    t = min(1024, s)
    assert s % t == 0
    return pl.pallas_call(
        _rope_kernel,
        grid=(s // t,),
        in_specs=[pl.BlockSpec((1, LANES), lambda i: (0, 0))],
        out_specs=[pl.BlockSpec((t, LANES), lambda i: (i, 0))] * 2,
        out_shape=[jax.ShapeDtypeStruct((s, LANES), F32)] * 2,
        name="rope_tables",
    )(inv)


def _rope_lane_perm(d):
    half = ATTN_HEAD_DIM // 2
    lane = jnp.arange(LANES)
    col_in_pair = ((lane // half) % 2) * ATTN_HEAD_DIM + (lane // (2 * half)) * half + lane % half
    return (jnp.arange(d // LANES)[:, None] * LANES + col_in_pair[None, :]).reshape(-1)


def _qkv_kernel(h_ref, g_ref, w_ref, qg_ref, kg_ref, cos_ref, sin_ref, same_head_ref, *refs):
    out_refs, (z_ref, stage_ref, tmp_ref) = refs[:-3], refs[-3:]
    t, d = h_ref.shape[1], h_ref.shape[2]

    @pl.when(pl.program_id(0) == 0)
    def _():
        z_ref[...] = jnp.zeros_like(z_ref)
    d1, d2 = DILATIONS[1], DILATIONS[2]
    n1, n2 = t // d1, t // d2
    group = same_head_ref.shape[0]
    hn = _rms(h_ref[0], g_ref[...]).astype(BF16)
    cos = cos_ref[...]
    sin = sin_ref[...]
    emitted = [0]

    def emit(xb, outs, hp):
        nat_ref, l1_ref, l2_ref = outs
        slot = emitted[0] % stage_ref.shape[0]
        emitted[0] += 1
        nat_ref[0, hp] = xb.astype(BF16)
        stage_ref[slot] = xb
        for r in range(d1):
            y = stage_ref[slot, pl.ds(r, n1, stride=d1), :]
            l1_ref[0, hp, r] = y.astype(BF16)
            tmp_ref[slot, r * n1:(r + 1) * n1, :] = y
        for r in range(d2):
            y = tmp_ref[slot, pl.ds((r % d1) * n1 + r // d1, n2, stride=d2 // d1), :]
            l2_ref[0, hp, r] = y.astype(BF16)

    for c0 in range(0, 3 * d, group):
        zc = z_ref[:, c0:c0 + group]
        z_ref[:, c0:c0 + group] = _dot(hn, w_ref[:, c0:c0 + group])
        which, col = divmod(c0, d)
        outs = out_refs[3 * which:3 * which + 3]
        if which < 2:
            ss = _dot((zc * zc).astype(BF16), same_head_ref[...])
            gain = (qg_ref, kg_ref)[which][:, col:col + group]
            zc = zc * lax.rsqrt(ss * (1.0 / ATTN_HEAD_DIM) + NORM_EPS) * gain
        for j in range(group // LANES):
            xb = zc[:, j * LANES:(j + 1) * LANES]
            if which < 2:
                xb = xb * cos + pltpu.roll(xb, LANES // 2, 1) * sin
            emit(xb, outs, (col + j * LANES) // LANES)


def _qkv(h, g, w_qkv, q_gain, k_gain, cos, sin):
    bsz, s, d = h.shape
    t = min(TOKEN_TILE, s)
    assert s % t == 0 and w_qkv.shape[1] == 3 * d and d % LANES == 0
    n_pairs = d // LANES
    d1, d2 = DILATIONS[1], DILATIONS[2]
    assert t % (d2 * 16) == 0
    half = ATTN_HEAD_DIM // 2

    def rope_order(cols):
        lead = cols.shape[:-1]
        x = cols.reshape(lead + (n_pairs, 2, 2, half))
        return jnp.swapaxes(x, -3, -2).reshape(lead + (d,))

    w = jnp.concatenate([rope_order(w_qkv[:, :d]), rope_order(w_qkv[:, d:2 * d]), w_qkv[:, 2 * d:]], axis=1)
    q_gain_lanes = rope_order(jnp.tile(q_gain * (ATTN_HEAD_DIM ** -0.5 * LOG2_E), d // ATTN_HEAD_DIM))
    k_gain_lanes = rope_order(jnp.tile(k_gain, d // ATTN_HEAD_DIM))
    group = 2 * LANES
    head_of_lane = _rope_lane_perm(d)[:group] // ATTN_HEAD_DIM
    same_head = (head_of_lane[:, None] == head_of_lane[None, :]).astype(BF16)
    shapes = [(bsz, n_pairs, s, LANES), (bsz, n_pairs, d1, s // d1, LANES), (bsz, n_pairs, d2, s // d2, LANES)]
    n_seq = s // t
    n_tiles = bsz * n_seq
    proj = lambda j: jnp.minimum(j, n_tiles - 1)
    done = lambda j: jnp.maximum(j - 1, 0)
    specs = [
        pl.BlockSpec((1, n_pairs, t, LANES), lambda j: (done(j) // n_seq, 0, done(j) % n_seq, 0)),
        pl.BlockSpec((1, n_pairs, d1, t // d1, LANES), lambda j: (done(j) // n_seq, 0, 0, done(j) % n_seq, 0)),
        pl.BlockSpec((1, n_pairs, d2, t // d2, LANES), lambda j: (done(j) // n_seq, 0, 0, done(j) % n_seq, 0)),
    ]
    outs = pl.pallas_call(
        _qkv_kernel,
        grid=(n_tiles + 1,),
        in_specs=[
            pl.BlockSpec((1, t, d), lambda j: (proj(j) // n_seq, proj(j) % n_seq, 0)),
            _const_spec((1, d)),
            _const_spec((d, 3 * d)),
            _const_spec((1, d)),
            _const_spec((1, d)),
            pl.BlockSpec((t, LANES), lambda j: (done(j) % n_seq, 0)),
            pl.BlockSpec((t, LANES), lambda j: (done(j) % n_seq, 0)),
            _const_spec((group, group)),
        ],
        out_specs=specs * 3,
        out_shape=[jax.ShapeDtypeStruct(sh, BF16) for sh in shapes] * 3,
        scratch_shapes=[pltpu.VMEM((t, 3 * d), F32), pltpu.VMEM((2, t, LANES), F32),
                        pltpu.VMEM((2, t, LANES), F32)],
        compiler_params=pltpu.CompilerParams(
            dimension_semantics=("arbitrary",), vmem_limit_bytes=VMEM_LIMIT_BYTES),
        name="qkv",
    )(h, g.reshape(1, d), w.astype(BF16), q_gain_lanes.reshape(1, d), k_gain_lanes.reshape(1, d), cos, sin,
      same_head)
    return outs[0:3], outs[3:6], outs[6:9]


def _attn_kernel(q1_ref, q4_ref, q16_ref, k1_ref, k4_ref, k16_ref, v1_ref, v4_ref, v16_ref, out_ref,
                 bias_ref, acc_ref, m_ref, l_ref, merged_ref):
    tq = q1_ref.shape[2]
    bq = ATTN_BLOCK_Q
    w = ATTN_WINDOW_STEPS
    nk = bq + w
    t0 = pl.program_id(2) * tq
    d1 = DILATIONS[1]

    def rows_of(ref, res, start, size):
        if len(ref.shape) == 4:
            return ref[0, 0, pl.ds(start, size), :]
        return ref[0, 0, res, pl.ds(start, size), :]

    i_minus_j = (lax.broadcasted_iota(jnp.int32, (bq, nk), 0) - lax.broadcasted_iota(jnp.int32, (bq, nk), 1))
    for variant, koff in enumerate((w, 0)):
        dist = i_minus_j + koff
        bias_ref[variant] = jnp.where((dist >= 0) & (dist <= w), 0.0, MASK_VALUE)

    lane = lax.broadcasted_iota(jnp.int32, (bq, LANES), 1)
    head0 = lane < ATTN_HEAD_DIM
    q_head0 = jnp.where((lane // (ATTN_HEAD_DIM // 2)) % 2 == 0, 1.0, 0.0).astype(BF16)
    q_head1 = (1.0 - q_head0.astype(F32)).astype(BF16)
    ones = jnp.ones((nk, LANES), BF16)

    for br, (dil, q_ref, k_ref, v_ref) in enumerate(
            zip(DILATIONS, (q1_ref, q4_ref, q16_ref), (k1_ref, k4_ref, k16_ref), (v1_ref, v4_ref, v16_ref))):
        blocks_per_res = tq // (dil * bq)
        for res in range(dil):
            for blk in range(blocks_per_res):
                l_k0 = t0 // dil + bq * blk - w
                variant = (l_k0 < 0).astype(jnp.int32)
                row0 = pl.multiple_of(jnp.maximum(l_k0, 0), bq)
                qb = rows_of(q_ref, res, blk * bq, bq)
                kb = rows_of(k_ref, res, row0, nk)
                vb = rows_of(v_ref, res, row0, nk)
                qq = jnp.concatenate([qb * q_head0, qb * q_head1], axis=0)
                s = _dot_nt(qq, kb)
                bias = bias_ref[variant]
                s = jnp.concatenate([s[:bq] + bias, s[bq:] + bias], axis=0)
                m = jnp.max(s, axis=-1, keepdims=True)
                p = jnp.exp2(s - m).astype(BF16)
                pv = _dot(p, jnp.concatenate([vb, ones], axis=1))
                if dil == 1:
                    rows = pl.ds(blk * bq, bq)
                else:
                    rows = pl.ds((res % d1) * (tq // d1) + res // d1 + (dil // d1) * bq * blk, bq, stride=dil // d1)
                acc_ref[br, rows, :] = jnp.where(head0, pv[:bq, :LANES], pv[bq:, :LANES])
                l_ref[br, rows, :] = jnp.where(head0, pv[:bq, LANES:], pv[bq:, LANES:])
                m_ref[br, rows, :] = jnp.where(head0, m[:bq], m[bq:])

    n1 = tq // d1
    for r in range(d1):
        tok = pl.ds(r, n1, stride=d1)
        res_rows = pl.ds(r * n1, n1)
        ms = [m_ref[0, tok, :]] + [m_ref[br, res_rows, :] for br in range(1, len(DILATIONS))]
        m_all = functools.reduce(jnp.maximum, ms)
        num = jnp.zeros((n1, LANES), F32)
        den = jnp.zeros((n1, LANES), F32)
        for br, m_br in enumerate(ms):
            wgt = jnp.exp2(m_br - m_all)
            src = tok if br == 0 else res_rows
            num = num + wgt * acc_ref[br, src, :]
            den = den + wgt * l_ref[br, src, :]
        merged_ref[tok, :] = num / den
    out_ref[0] = merged_ref[...].astype(out_ref.dtype)


def _attention(q, k, v):
    bsz, n_pairs, s, _ = q[0].shape
    tq = ATTN_Q_TILE
    assert s % tq == 0 and tq == ATTN_BLOCK_Q * max(DILATIONS) and s >= 2 * tq
    assert len(DILATIONS) == 3 and DILATIONS[0] == 1
    nb = len(DILATIONS)

    def spec(dil, rows, whole):
        if dil == 1:
            return pl.BlockSpec((1, 1, rows, LANES), lambda b, hp, i: (b, hp, 0 if whole else i, 0))
        return pl.BlockSpec((1, 1, dil, rows, LANES), lambda b, hp, i: (b, hp, 0, 0 if whole else i, 0))

    q_specs = [spec(dil, tq // dil, False) for dil in DILATIONS]
    kv_specs = [spec(dil, s // dil, True) for dil in DILATIONS]
    return pl.pallas_call(
        _attn_kernel,
        grid=(bsz, n_pairs, s // tq),
        in_specs=q_specs + kv_specs + kv_specs,
        out_specs=pl.BlockSpec((1, tq, LANES), lambda b, hp, i: (b, i, hp)),
        out_shape=jax.ShapeDtypeStruct((bsz, s, n_pairs * LANES), BF16),
        scratch_shapes=[
            pltpu.VMEM((2, ATTN_BLOCK_Q, ATTN_BLOCK_Q + ATTN_WINDOW_STEPS), F32),
        ] + [pltpu.VMEM((nb, tq, LANES), F32)] * 3 + [pltpu.VMEM((tq, LANES), F32)],
        compiler_params=pltpu.CompilerParams(
            dimension_semantics=("arbitrary", "arbitrary", "arbitrary"), vmem_limit_bytes=VMEM_LIMIT_BYTES),
        name="dilated_attention",
    )(*q, *k, *v)


def kernel(x, p, mix_norm, w_in_ab, pool_w, pool_scale, hgrn_lb, hgrn_o_norm, w_out_ab, w_qkv, q_norm, k_norm,
           w_o, mlp_norm, w_up, w_down, ple_norm, w_ple, w_ple_gate):
    depth = p.shape[0]
    s = x.shape[1]
    h = x
    cos = sin = None
    for layer in range(depth):
        if layer % 2 == 0:
            e = layer // 2
            mix = _mixer0(h, mix_norm[layer], w_in_ab[e], pool_w[e], pool_scale[e], hgrn_lb, hgrn_o_norm[e], layer)
            w_mix = w_out_ab[e]
        else:
            ci = layer // 2
            if cos is None:
                cos, sin = _rope_tables(s)
            q, k, v = _qkv(h, mix_norm[layer], w_qkv[ci], q_norm[ci], k_norm[ci], cos, sin)
            mix = _attention(q, k, v)
            w_mix = w_o[ci]
        h = _post(h, mix, p, layer, w_mix, mlp_norm[layer], w_up[layer], w_down[layer], ple_norm[layer],
                  w_ple_gate[layer], w_ple[layer])
    return h
```

```python
import functools

import jax
import jax.numpy as jnp
from jax import lax
from jax.experimental import pallas as pl
from jax.experimental.pallas import tpu as pltpu

F32 = jnp.float32
BF16 = jnp.bfloat16

NORM_EPS = 1e-6
POOL_WINDOWS = (2, 4, 8, 16)
POOL_GROUP_DIM = 128
POOL_CARRY_ROWS = 16
HGRN_HEAD_DIM = 128
HGRN_CHUNK = 128
HGRN_DIAG = 8
HGRN_GROUP = 8
HGRN_BOUNDED_LOG2_RANGE = 120.0
ATTN_HEAD_DIM = 64
ATTN_WINDOW_STEPS = 128
DILATIONS = (1, 4, 16)
ATTN_BLOCK_Q = 128
ROPE_THETA = 10000.0
LANES = 128
MASK_VALUE = -1e30
LOG2_E = 1.4426950408889634

VMEM_LIMIT_BYTES = 56 * 1024 * 1024

TOKEN_TILE = 512
ATTN_Q_TILE = 2048
ATTN_MERGE_ROWS = 32


def _dot(a, b):
    return jnp.dot(a, b, preferred_element_type=F32)


def _dot_nt(a, b):
    return lax.dot_general(a, b, (((1,), (1,)), ((), ())), preferred_element_type=F32)


def _dot_tn(a, b):
    return lax.dot_general(a, b, (((0,), (0,)), ((), ())), preferred_element_type=F32)


def _rms(x, g):
    ms = jnp.mean(x * x, axis=-1, keepdims=True)
    return x * lax.rsqrt(ms + NORM_EPS) * g


def _sigmoid(x):
    return 1.0 / (1.0 + jnp.exp(-x))


def _const_spec(shape):
    nd = len(shape)
    return pl.BlockSpec(shape, lambda *_: (0,) * nd, pipeline_mode=pl.Buffered(1))


def _hgrn_diag_scores(qb, bb, b_row, k_row, lane_mod):
    n = qb.shape[0]
    scores = jnp.zeros((n, HGRN_CHUNK), F32)
    for s in range(n):
        e = jnp.exp2(bb - b_row(s))
        a = jnp.sum(qb * e * k_row(s), axis=-1, keepdims=True)
        scores = jnp.where(lane_mod == s, a, scores)
    return scores


def _hgrn_level_scores(q, kk, b, span):
    c = q.shape[0]
    zeros = jnp.zeros((span, q.shape[1]), F32)
    q_parts, k_parts = [], []
    for start in range(0, c, 2 * span):
        mid = start + span
        anchor = b[mid - 1:mid, :]
        lo = slice(start, mid)
        hi = slice(mid, mid + span)
        q_parts += [zeros, q[hi] * jnp.exp2(b[hi] - anchor)]
        k_parts += [kk[lo] * jnp.exp2(anchor - b[lo]), zeros]
    qe = jnp.concatenate(q_parts, axis=0).astype(BF16)
    ke = jnp.concatenate(k_parts, axis=0).astype(BF16)
    return _dot_nt(qe, ke)


def _hgrn_chunk_head(q, kk, v, b, st, masks, b_row, k_row):
    c = HGRN_CHUNK
    d = HGRN_DIAG
    causal_diag, same_parent, lane_mod = masks
    o_inter = _dot_nt((q * jnp.exp2(b)).astype(BF16), st.astype(BF16))
    diag = jnp.concatenate(
        [_hgrn_diag_scores(q[i:i + d], b[i:i + d], lambda s, i=i: b_row(i + s), lambda s, i=i: k_row(i + s),
                           lane_mod) for i in range(0, c, d)], axis=0)
    span = c // 2
    scores = _hgrn_level_scores(q, kk, b, span)
    while span > d:
        span //= 2
        scores = jnp.where(same_parent[2 * span], _hgrn_level_scores(q, kk, b, span), scores)
    scores = jnp.where(same_parent[d], jnp.where(causal_diag, diag, 0.0), scores)
    o = o_inter + _dot(scores.astype(BF16), v.astype(BF16))
    b_last = b[c - 1:c, :]
    ke = (kk * jnp.exp2(b_last - b)).astype(BF16)
    st_new = jnp.exp2(b_last) * st + _dot_tn(v.astype(BF16), ke)
    return o, st_new


def _hgrn_chunk_head_bounded(q, kk, v, b, st, causal):
    c = HGRN_CHUNK
    b_mid = b[c // 2 - 1:c // 2, :]
    qe = (q * jnp.exp2(b - b_mid)).astype(BF16)
    ke = (kk * jnp.exp2(b_mid - b)).astype(BF16)
    scores = jnp.where(causal, _dot_nt(qe, ke), 0.0)
    o = _dot_nt((q * jnp.exp2(b)).astype(BF16), st.astype(BF16)) + _dot(scores.astype(BF16), v.astype(BF16))
    b_last = b[c - 1:c, :]
    ks = (kk * jnp.exp2(b_last - b)).astype(BF16)
    st_new = jnp.exp2(b_last) * st + _dot_tn(v.astype(BF16), ks)
    return o, st_new


def _hgrn_masks():
    c = HGRN_CHUNK
    r = lax.broadcasted_iota(jnp.int32, (c, c), 0)
    s = lax.broadcasted_iota(jnp.int32, (c, c), 1)
    same_parent = {}
    n = HGRN_DIAG
    while n < c:
        shift = n.bit_length() - 1
        same_parent[n] = lax.shift_right_logical(r, shift) == lax.shift_right_logical(s, shift)
        n *= 2
    lane_mod = lax.broadcasted_iota(jnp.int32, (HGRN_DIAG, c), 1) & (HGRN_DIAG - 1)
    return r >= s, same_parent, lane_mod


def _mixer0_kernel(h_ref, g_ref, win_ref, poolw_ref, pscale_ref, lb_ref, onorm_ref, out_ref,
                   z_ref, upad_ref, state_ref, brow_ref, krow_ref, *, layer):
    t = h_ref.shape[1]
    si = pl.program_id(1)
    pool_w = len(POOL_WINDOWS) * POOL_GROUP_DIM
    n_heads = state_ref.shape[0]
    hw = n_heads * HGRN_HEAD_DIM

    @pl.when(si == 0)
    def _():
        state_ref[...] = jnp.zeros_like(state_ref)
        upad_ref[0:POOL_CARRY_ROWS, :] = jnp.zeros((POOL_CARRY_ROWS, pool_w), F32)

    hn = _rms(h_ref[0], g_ref[...]).astype(BF16)
    z_ref[...] = _dot(hn, win_ref[...])

    upad_ref[POOL_CARRY_ROWS:POOL_CARRY_ROWS + t, :] = z_ref[:, 0:pool_w]
    pos = lax.broadcasted_iota(jnp.int32, (t, 1), 0) + si * t
    for gi, w in enumerate(POOL_WINDOWS):
        cols = slice(gi * POOL_GROUP_DIM, (gi + 1) * POOL_GROUP_DIM)
        u = upad_ref[POOL_CARRY_ROWS:POOL_CARRY_ROWS + t, cols]
        acc = u
        for j in range(1, w):
            acc = acc + upad_ref[POOL_CARRY_ROWS - j:POOL_CARRY_ROWS - j + t, cols]
        cnt = jnp.minimum(pos + 1, w).astype(F32)
        dlt = acc / cnt - u
        y = _dot(dlt.astype(BF16), poolw_ref[gi]) * pscale_ref[:, cols]
        out_ref[0, :, cols] = y.astype(out_ref.dtype)
    upad_ref[0:POOL_CARRY_ROWS, :] = upad_ref[t:t + POOL_CARRY_ROWS, :]

    lbraw = lb_ref[...]
    lbe = jnp.exp(lbraw - jnp.max(lbraw, axis=0, keepdims=True))
    lbs = lbe / jnp.sum(lbe, axis=0, keepdims=True)
    lb = jnp.sum(lbs[0:layer + 1], axis=0, keepdims=True)
    c = HGRN_CHUNK
    ri = lax.broadcasted_iota(jnp.int32, (c, c), 0)
    ci = lax.broadcasted_iota(jnp.int32, (c, c), 1)
    ltri = jnp.where(ri >= ci, 1.0, 0.0).astype(BF16)
    onorm = onorm_ref[...]
    masks = _hgrn_masks()

    group = t // c
    head_cols = [slice(hd * HGRN_HEAD_DIM, (hd + 1) * HGRN_HEAD_DIM) for hd in range(n_heads)]

    zq, zi, zg, kk, b = [], [], [], [], []
    for u in range(group):
        rows = slice(u * c, (u + 1) * c)
        zq.append(z_ref[rows, pool_w:pool_w + hw])
        zf = z_ref[rows, pool_w + hw:pool_w + 2 * hw]
        zi.append(z_ref[rows, pool_w + 2 * hw:pool_w + 3 * hw])
        zg.append(z_ref[rows, pool_w + 3 * hw:pool_w + 4 * hw])
        f = lb + (1.0 - lb) * _sigmoid(zf)
        logf = jnp.log2(f)
        kk.append(1.0 - f)
        g_hi = logf.astype(BF16)
        r1 = logf - g_hi.astype(F32)
        g_mid = r1.astype(BF16)
        g_lo = (r1 - g_mid.astype(F32)).astype(BF16)
        b.append(_dot(ltri, g_hi) + _dot(ltri, g_mid) + _dot(ltri, g_lo))
    half_min = None
    for u in range(group):
        b_mid = b[u][c // 2 - 1:c // 2, :]
        worst = jnp.minimum(b_mid, b[u][c - 1:c, :] - b_mid)
        half_min = worst if half_min is None else jnp.minimum(half_min, worst)
    bounded = jnp.min(half_min) >= -HGRN_BOUNDED_LOG2_RANGE

    def run(chunk_head):
        states = [state_ref[hd] for hd in range(n_heads)]
        for u in range(group):
            for hd, cols in enumerate(head_cols):
                o, states[hd] = chunk_head(u, cols, states[hd])
                o = o * lax.rsqrt(jnp.mean(o * o, axis=-1, keepdims=True) + NORM_EPS) * onorm
                gate = zg[u][:, cols]
                o = o * (gate * _sigmoid(gate))
                out_ref[0, u * c:(u + 1) * c, pool_w + cols.start:pool_w + cols.stop] = o.astype(out_ref.dtype)
        for hd in range(n_heads):
            state_ref[hd] = states[hd]

    @pl.when(bounded)
    def _():
        run(lambda u, cols, st: _hgrn_chunk_head_bounded(
            zq[u][:, cols], kk[u][:, cols], zi[u][:, cols], b[u][:, cols], st, masks[0]))

    @pl.when(jnp.logical_not(bounded))
    def _():
        for u in range(group):
            brow_ref[u] = b[u]
            krow_ref[u] = kk[u]
        run(lambda u, cols, st: _hgrn_chunk_head(
            zq[u][:, cols], kk[u][:, cols], zi[u][:, cols], b[u][:, cols], st, masks,
            lambda i: brow_ref[u, i:i + 1, cols], lambda i: krow_ref[u, i:i + 1, cols]))


def _mixer0(h, g, w_in, pool_w, pool_scale, hgrn_lb, o_norm, layer):
    bsz, s, d = h.shape
    t = min(TOKEN_TILE, s)
    assert s % t == 0 and t % HGRN_CHUNK == 0
    zw = w_in.shape[1]
    pw = len(POOL_WINDOWS) * POOL_GROUP_DIM
    hw = hgrn_lb.shape[1]
    assert zw == pw + 4 * hw and hw % HGRN_HEAD_DIM == 0
    n_heads = hw // HGRN_HEAD_DIM
    return pl.pallas_call(
        functools.partial(_mixer0_kernel, layer=layer),
        grid=(bsz, s // t),
        in_specs=[
            pl.BlockSpec((1, t, d), lambda b, i: (b, i, 0)),
            _const_spec((1, d)),
            _const_spec((d, zw)),
            _const_spec(pool_w.shape),
            _const_spec((1, pw)),
            _const_spec(hgrn_lb.shape),
            _const_spec((1, HGRN_HEAD_DIM)),
        ],
        out_specs=pl.BlockSpec((1, t, pw + hw), lambda b, i: (b, i, 0)),
        out_shape=jax.ShapeDtypeStruct((bsz, s, pw + hw), BF16),
        scratch_shapes=[
            pltpu.VMEM((t, zw), F32),
            pltpu.VMEM((t + POOL_CARRY_ROWS, pw), F32),
            pltpu.VMEM((n_heads, HGRN_HEAD_DIM, HGRN_HEAD_DIM), F32),
            pltpu.VMEM((t // HGRN_CHUNK, HGRN_CHUNK, hw), F32),
            pltpu.VMEM((t // HGRN_CHUNK, HGRN_CHUNK, hw), F32),
        ],
        compiler_params=pltpu.CompilerParams(
            dimension_semantics=("arbitrary", "arbitrary"), vmem_limit_bytes=VMEM_LIMIT_BYTES),
        name="mixer0",
    )(h, g.reshape(1, d), w_in.astype(BF16), pool_w.astype(BF16), pool_scale.reshape(1, pw),
      hgrn_lb, o_norm.reshape(1, HGRN_HEAD_DIM))


def _post_kernel(h_ref, mix_ref, p_ref, wmix_ref, gm_ref, wup_ref, wdn_ref, gp_ref, wg_ref, wple_ref, out_ref,
                 *, ff_chunk):
    h1 = h_ref[0] + _dot(mix_ref[0], wmix_ref[...])
    hm = _rms(h1, gm_ref[...]).astype(BF16)
    d_ff = wup_ref.shape[1]
    acc = jnp.zeros_like(h1)
    for c0 in range(0, d_ff, ff_chunk):
        up = _dot(hm, wup_ref[:, c0:c0 + ff_chunk])
        act = jnp.square(jnp.maximum(up, 0.0)).astype(BF16)
        acc = acc + _dot(act, wdn_ref[c0:c0 + ff_chunk, :])
    h2 = h1 + acc
    gate = _sigmoid(_dot(_rms(h2, gp_ref[...]).astype(BF16), wg_ref[...]))
    pe = _dot(p_ref[0, 0].astype(BF16), wple_ref[...])
    out_ref[0] = h2 + pe * gate


def _post(h, mix, p, layer, w_mix, g_mlp, w_up, w_down, g_ple, w_gate, w_ple):
    bsz, s, d = h.shape
    t = min(TOKEN_TILE, s)
    assert s % t == 0
    d_mix = mix.shape[-1]
    d_ff = w_up.shape[1]
    d_p = p.shape[-1]
    ff_chunk = min(1024, d_ff)
    assert d_ff % ff_chunk == 0
    tok = lambda width: pl.BlockSpec((1, t, width), lambda b, i: (b, i, 0))
    return pl.pallas_call(
        functools.partial(_post_kernel, ff_chunk=ff_chunk),
        grid=(bsz, s // t),
        in_specs=[
            tok(d), tok(d_mix),
            pl.BlockSpec((1, 1, t, d_p), lambda b, i: (layer, b, i, 0)),
            _const_spec((d_mix, d)),
            _const_spec((1, d)),
            _const_spec((d, d_ff)),
            _const_spec((d_ff, d)),
            _const_spec((1, d)),
            _const_spec((d, d)),
            _const_spec((d_p, d)),
        ],
        out_specs=tok(d),
        out_shape=jax.ShapeDtypeStruct((bsz, s, d), F32),
        compiler_params=pltpu.CompilerParams(
            dimension_semantics=("arbitrary", "arbitrary"), vmem_limit_bytes=VMEM_LIMIT_BYTES),
        name="post",
    )(h, mix, p, w_mix.astype(BF16), g_mlp.reshape(1, d), w_up.astype(BF16), w_down.astype(BF16),
      g_ple.reshape(1, d), w_gate.astype(BF16), w_ple.astype(BF16))


def _rope_kernel(inv_ref, cos_ref, sin_ref):
    t = cos_ref.shape[0]
    pos = (lax.broadcasted_iota(jnp.int32, (t, LANES), 0) + pl.program_id(0) * t).astype(F32)
    lane = lax.broadcasted_iota(jnp.int32, (t, LANES), 1)
    ang = pos * inv_ref[...]
    cos_ref[...] = jnp.cos(ang)
    sin = jnp.sin(ang)
    sin_ref[...] = jnp.where(lane < LANES // 2, -sin, sin)


def _rope_tables(s):
    half = ATTN_HEAD_DIM // 2
    inv = ROPE_THETA ** (-jnp.arange(half, dtype=F32) / half)
    inv = jnp.tile(inv, LANES // half).reshape(1, LANES)
    t = min(1024, s)
    assert s % t == 0
    return pl.pallas_call(
        _rope_kernel,
        grid=(s // t,),
        in_specs=[pl.BlockSpec((1, LANES), lambda i: (0, 0))],
        out_specs=[pl.BlockSpec((t, LANES), lambda i: (i, 0))] * 2,
        out_shape=[jax.ShapeDtypeStruct((s, LANES), F32)] * 2,
        name="rope_tables",
    )(inv)


def _rope_lane_perm(d):
    half = ATTN_HEAD_DIM // 2
    lane = jnp.arange(LANES)
    col_in_pair = ((lane // half) % 2) * ATTN_HEAD_DIM + (lane // (2 * half)) * half + lane % half
    return (jnp.arange(d // LANES)[:, None] * LANES + col_in_pair[None, :]).reshape(-1)


def _qkv_kernel(h_ref, g_ref, w_ref, qg_ref, kg_ref, cos_ref, sin_ref, same_head_ref, *refs):
    out_refs, (z_ref, stage_ref, tmp_ref) = refs[:-3], refs[-3:]
    t, d = h_ref.shape[1], h_ref.shape[2]

    @pl.when(pl.program_id(0) == 0)
    def _():
        z_ref[...] = jnp.zeros_like(z_ref)
    d1, d2 = DILATIONS[1], DILATIONS[2]
    n1, n2 = t // d1, t // d2
    group = same_head_ref.shape[0]
    hn = _rms(h_ref[0], g_ref[...]).astype(BF16)
    cos = cos_ref[...]
    sin = sin_ref[...]
    emitted = [0]

    def emit(xb, outs, hp):
        nat_ref, l1_ref, l2_ref = outs
        slot = emitted[0] % stage_ref.shape[0]
        emitted[0] += 1
        nat_ref[0, hp] = xb.astype(BF16)
        stage_ref[slot] = xb
        for r in range(d1):
            y = stage_ref[slot, pl.ds(r, n1, stride=d1), :]
            l1_ref[0, hp, r] = y.astype(BF16)
            tmp_ref[slot, r * n1:(r + 1) * n1, :] = y
        for r in range(d2):
            y = tmp_ref[slot, pl.ds((r % d1) * n1 + r // d1, n2, stride=d2 // d1), :]
            l2_ref[0, hp, r] = y.astype(BF16)

    for c0 in range(0, 3 * d, group):
        zc = z_ref[:, c0:c0 + group]
        z_ref[:, c0:c0 + group] = _dot(hn, w_ref[:, c0:c0 + group])
        which, col = divmod(c0, d)
        outs = out_refs[3 * which:3 * which + 3]
        if which < 2:
            ss = _dot((zc * zc).astype(BF16), same_head_ref[...])
            gain = (qg_ref, kg_ref)[which][:, col:col + group]
            zc = zc * lax.rsqrt(ss * (1.0 / ATTN_HEAD_DIM) + NORM_EPS) * gain
        for j in range(group // LANES):
            xb = zc[:, j * LANES:(j + 1) * LANES]
            if which < 2:
                xb = xb * cos + pltpu.roll(xb, LANES // 2, 1) * sin
            emit(xb, outs, (col + j * LANES) // LANES)


def _qkv(h, g, w_qkv, q_gain, k_gain, cos, sin):
    bsz, s, d = h.shape
    t = min(TOKEN_TILE, s)
    assert s % t == 0 and w_qkv.shape[1] == 3 * d and d % LANES == 0
    n_pairs = d // LANES
    d1, d2 = DILATIONS[1], DILATIONS[2]
    assert t % (d2 * 16) == 0
    half = ATTN_HEAD_DIM // 2

    def rope_order(cols):
        lead = cols.shape[:-1]
        x = cols.reshape(lead + (n_pairs, 2, 2, half))
        return jnp.swapaxes(x, -3, -2).reshape(lead + (d,))

    w = jnp.concatenate([rope_order(w_qkv[:, :d]), rope_order(w_qkv[:, d:2 * d]), w_qkv[:, 2 * d:]], axis=1)
    q_gain_lanes = rope_order(jnp.tile(q_gain * (ATTN_HEAD_DIM ** -0.5 * LOG2_E), d // ATTN_HEAD_DIM))
    k_gain_lanes = rope_order(jnp.tile(k_gain, d // ATTN_HEAD_DIM))
    group = 2 * LANES
    head_of_lane = _rope_lane_perm(d)[:group] // ATTN_HEAD_DIM
    same_head = (head_of_lane[:, None] == head_of_lane[None, :]).astype(BF16)
    shapes = [(bsz, n_pairs, s, LANES), (bsz, n_pairs, d1, s // d1, LANES), (bsz, n_pairs, d2, s // d2, LANES)]
    n_seq = s // t
    n_tiles = bsz * n_seq
    proj = lambda j: jnp.minimum(j, n_tiles - 1)
    done = lambda j: jnp.maximum(j - 1, 0)
    specs = [
        pl.BlockSpec((1, n_pairs, t, LANES), lambda j: (done(j) // n_seq, 0, done(j) % n_seq, 0)),
        pl.BlockSpec((1, n_pairs, d1, t // d1, LANES), lambda j: (done(j) // n_seq, 0, 0, done(j) % n_seq, 0)),
        pl.BlockSpec((1, n_pairs, d2, t // d2, LANES), lambda j: (done(j) // n_seq, 0, 0, done(j) % n_seq, 0)),
    ]
    outs = pl.pallas_call(
        _qkv_kernel,
        grid=(n_tiles + 1,),
        in_specs=[
            pl.BlockSpec((1, t, d), lambda j: (proj(j) // n_seq, proj(j) % n_seq, 0)),
            _const_spec((1, d)),
            _const_spec((d, 3 * d)),
            _const_spec((1, d)),
            _const_spec((1, d)),
            pl.BlockSpec((t, LANES), lambda j: (done(j) % n_seq, 0)),
            pl.BlockSpec((t, LANES), lambda j: (done(j) % n_seq, 0)),
            _const_spec((group, group)),
        ],
        out_specs=specs * 3,
        out_shape=[jax.ShapeDtypeStruct(sh, BF16) for sh in shapes] * 3,
        scratch_shapes=[pltpu.VMEM((t, 3 * d), F32), pltpu.VMEM((2, t, LANES), F32),
                        pltpu.VMEM((2, t, LANES), F32)],
        compiler_params=pltpu.CompilerParams(
            dimension_semantics=("arbitrary",), vmem_limit_bytes=VMEM_LIMIT_BYTES),
        name="qkv",
    )(h, g.reshape(1, d), w.astype(BF16), q_gain_lanes.reshape(1, d), k_gain_lanes.reshape(1, d), cos, sin,
      same_head)
    return outs[0:3], outs[3:6], outs[6:9]


def _attn_kernel(q1_ref, q4_ref, q16_ref, k1_ref, k4_ref, k16_ref, v1_ref, v4_ref, v16_ref, out_ref,
                 bias_ref, acc_ref, m_ref, l_ref, merged_ref):
    tq = q1_ref.shape[2]
    bq = ATTN_BLOCK_Q
    w = ATTN_WINDOW_STEPS
    nk = bq + w
    t0 = pl.program_id(2) * tq
    d1 = DILATIONS[1]

    def rows_of(ref, res, start, size):
        if len(ref.shape) == 4:
            return ref[0, 0, pl.ds(start, size), :]
        return ref[0, 0, res, pl.ds(start, size), :]

    i_minus_j = (lax.broadcasted_iota(jnp.int32, (bq, nk), 0) - lax.broadcasted_iota(jnp.int32, (bq, nk), 1))
    for variant, koff in enumerate((w, 0)):
        dist = i_minus_j + koff
        bias_ref[variant] = jnp.where((dist >= 0) & (dist <= w), 0.0, MASK_VALUE)

    lane = lax.broadcasted_iota(jnp.int32, (bq, LANES), 1)
    head0 = lane < ATTN_HEAD_DIM
    q_head0 = jnp.where((lane // (ATTN_HEAD_DIM // 2)) % 2 == 0, 1.0, 0.0).astype(BF16)
    q_head1 = (1.0 - q_head0.astype(F32)).astype(BF16)
    ones = jnp.ones((nk, LANES), BF16)

    for br, (dil, q_ref, k_ref, v_ref) in enumerate(
            zip(DILATIONS, (q1_ref, q4_ref, q16_ref), (k1_ref, k4_ref, k16_ref), (v1_ref, v4_ref, v16_ref))):
        blocks_per_res = tq // (dil * bq)
        for res in range(dil):
            for blk in range(blocks_per_res):
                l_k0 = t0 // dil + bq * blk - w
                variant = (l_k0 < 0).astype(jnp.int32)
                row0 = pl.multiple_of(jnp.maximum(l_k0, 0), bq)
                qb = rows_of(q_ref, res, blk * bq, bq)
                kb = rows_of(k_ref, res, row0, nk)
                vb = rows_of(v_ref, res, row0, nk)
                qq = jnp.concatenate([qb * q_head0, qb * q_head1], axis=0)
                s = _dot_nt(qq, kb)
                bias = bias_ref[variant]
                s = jnp.concatenate([s[:bq] + bias, s[bq:] + bias], axis=0)
                m = jnp.max(s, axis=-1, keepdims=True)
                p = jnp.exp2(s - m).astype(BF16)
                pv = _dot(p, jnp.concatenate([vb, ones], axis=1))
                if dil == 1:
                    rows = pl.ds(blk * bq, bq)
                else:
                    rows = pl.ds((res % d1) * (tq // d1) + res // d1 + (dil // d1) * bq * blk, bq, stride=dil // d1)
                acc_ref[br, rows, :] = jnp.where(head0, pv[:bq, :LANES], pv[bq:, :LANES])
                l_ref[br, rows, :] = jnp.where(head0, pv[:bq, LANES:], pv[bq:, LANES:])
                m_ref[br, rows, :] = jnp.where(head0, m[:bq], m[bq:])

    n1 = tq // d1
    rows_per_step = ATTN_MERGE_ROWS

    def merge_step(idx, carry, r):
        c0 = idx * rows_per_step
        tok = pl.ds(r + d1 * c0, rows_per_step, stride=d1)
        res_rows = pl.ds(pl.multiple_of(r * n1 + c0, rows_per_step), rows_per_step)
        ms = [m_ref[0, tok, :]] + [m_ref[br, res_rows, :] for br in range(1, len(DILATIONS))]
        m_all = functools.reduce(jnp.maximum, ms)
        num = jnp.zeros((rows_per_step, LANES), F32)
        den = jnp.zeros((rows_per_step, LANES), F32)
        for br, m_br in enumerate(ms):
            wgt = jnp.exp2(m_br - m_all)
            src = tok if br == 0 else res_rows
            num = num + wgt * acc_ref[br, src, :]
            den = den + wgt * l_ref[br, src, :]
        merged_ref[tok, :] = num / den
        return carry

    for r in range(d1):
        lax.fori_loop(0, n1 // rows_per_step, functools.partial(merge_step, r=r), 0, unroll=8)
    out_ref[0] = merged_ref[...].astype(out_ref.dtype)


def _attention(q, k, v):
    bsz, n_pairs, s, _ = q[0].shape
    tq = ATTN_Q_TILE
    assert s % tq == 0 and tq == ATTN_BLOCK_Q * max(DILATIONS) and s >= 2 * tq
    assert len(DILATIONS) == 3 and DILATIONS[0] == 1
    nb = len(DILATIONS)

    def spec(dil, rows, whole):
        if dil == 1:
            return pl.BlockSpec((1, 1, rows, LANES), lambda b, hp, i: (b, hp, 0 if whole else i, 0))
        return pl.BlockSpec((1, 1, dil, rows, LANES), lambda b, hp, i: (b, hp, 0, 0 if whole else i, 0))

    q_specs = [spec(dil, tq // dil, False) for dil in DILATIONS]
    kv_specs = [spec(dil, s // dil, True) for dil in DILATIONS]
    return pl.pallas_call(
        _attn_kernel,
        grid=(bsz, n_pairs, s // tq),
        in_specs=q_specs + kv_specs + kv_specs,
        out_specs=pl.BlockSpec((1, tq, LANES), lambda b, hp, i: (b, i, hp)),
        out_shape=jax.ShapeDtypeStruct((bsz, s, n_pairs * LANES), BF16),
        scratch_shapes=[
            pltpu.VMEM((2, ATTN_BLOCK_Q, ATTN_BLOCK_Q + ATTN_WINDOW_STEPS), F32),
        ] + [pltpu.VMEM((nb, tq, LANES), F32)] * 3 + [pltpu.VMEM((tq, LANES), F32)],
        compiler_params=pltpu.CompilerParams(
            dimension_semantics=("arbitrary", "arbitrary", "arbitrary"), vmem_limit_bytes=VMEM_LIMIT_BYTES),
        name="dilated_attention",
    )(*q, *k, *v)


def kernel(x, p, mix_norm, w_in_ab, pool_w, pool_scale, hgrn_lb, hgrn_o_norm, w_out_ab, w_qkv, q_norm, k_norm,
           w_o, mlp_norm, w_up, w_down, ple_norm, w_ple, w_ple_gate):
    depth = p.shape[0]
    s = x.shape[1]
    h = x
    cos = sin = None
    for layer in range(depth):
        if layer % 2 == 0:
            e = layer // 2
            mix = _mixer0(h, mix_norm[layer], w_in_ab[e], pool_w[e], pool_scale[e], hgrn_lb, hgrn_o_norm[e], layer)
            w_mix = w_out_ab[e]
        else:
            ci = layer // 2
            if cos is None:
                cos, sin = _rope_tables(s)
            q, k, v = _qkv(h, mix_norm[layer], w_qkv[ci], q_norm[ci], k_norm[ci], cos, sin)
            mix = _attention(q, k, v)
            w_mix = w_o[ci]
        h = _post(h, mix, p, layer, w_mix, mlp_norm[layer], w_up[layer], w_down[layer], ple_norm[layer],
                  w_ple_gate[layer], w_ple[layer])
    return h
```

```python
import functools

import jax
import jax.numpy as jnp
from jax import lax
from jax.experimental import pallas as pl
from jax.experimental.pallas import tpu as pltpu

F32 = jnp.float32
BF16 = jnp.bfloat16

NORM_EPS = 1e-6
POOL_WINDOWS = (2, 4, 8, 16)
POOL_GROUP_DIM = 128
POOL_CARRY_ROWS = 16
HGRN_HEAD_DIM = 128
HGRN_CHUNK = 128
HGRN_DIAG = 8
HGRN_BOUNDED_LOG2_RANGE = 120.0
ATTN_HEAD_DIM = 64
ATTN_WINDOW_STEPS = 128
DILATIONS = (1, 4, 16)
ATTN_BLOCK_Q = 128
ROPE_THETA = 10000.0
LANES = 128
MASK_VALUE = -1e30
LOG2_E = 1.4426950408889634

VMEM_LIMIT_BYTES = 56 * 1024 * 1024

TOKEN_TILE = 512
ATTN_Q_TILE = 2048
ATTN_MERGE_ROWS = 32


def _dot(a, b):
    return jnp.dot(a, b, preferred_element_type=F32)


def _dot_nt(a, b):
    return lax.dot_general(a, b, (((1,), (1,)), ((), ())), preferred_element_type=F32)


def _dot_tn(a, b):
    return lax.dot_general(a, b, (((0,), (0,)), ((), ())), preferred_element_type=F32)


def _rms(x, g):
    ms = jnp.mean(x * x, axis=-1, keepdims=True)
    return x * lax.rsqrt(ms + NORM_EPS) * g


def _sigmoid(x):
    return 1.0 / (1.0 + jnp.exp(-x))


def _const_spec(shape):
    nd = len(shape)
    return pl.BlockSpec(shape, lambda *_: (0,) * nd, pipeline_mode=pl.Buffered(1))


def _hgrn_diag_scores(qb, bb, b_row, k_row, lane_mod):
    n = qb.shape[0]
    scores = jnp.zeros((n, HGRN_CHUNK), F32)
    for s in range(n):
        e = jnp.exp2(bb - b_row(s))
        a = jnp.sum(qb * e * k_row(s), axis=-1, keepdims=True)
        scores = jnp.where(lane_mod == s, a, scores)
    return scores


def _hgrn_level_scores(q, kk, b, span):
    c = q.shape[0]
    zeros = jnp.zeros((span, q.shape[1]), F32)
    q_parts, k_parts = [], []
    for start in range(0, c, 2 * span):
        mid = start + span
        anchor = b[mid - 1:mid, :]
        lo = slice(start, mid)
        hi = slice(mid, mid + span)
        q_parts += [zeros, q[hi] * jnp.exp2(b[hi] - anchor)]
        k_parts += [kk[lo] * jnp.exp2(anchor - b[lo]), zeros]
    qe = jnp.concatenate(q_parts, axis=0).astype(BF16)
    ke = jnp.concatenate(k_parts, axis=0).astype(BF16)
    return _dot_nt(qe, ke)


def _hgrn_chunk_head(q, kk, v, b, st, masks, b_row, k_row):
    c = HGRN_CHUNK
    d = HGRN_DIAG
    causal_diag, same_parent, lane_mod = masks
    o_inter = _dot_nt((q * jnp.exp2(b)).astype(BF16), st.astype(BF16))
    diag = jnp.concatenate(
        [_hgrn_diag_scores(q[i:i + d], b[i:i + d], lambda s, i=i: b_row(i + s), lambda s, i=i: k_row(i + s),
                           lane_mod) for i in range(0, c, d)], axis=0)
    span = c // 2
    scores = _hgrn_level_scores(q, kk, b, span)
    while span > d:
        span //= 2
        scores = jnp.where(same_parent[2 * span], _hgrn_level_scores(q, kk, b, span), scores)
    scores = jnp.where(same_parent[d], jnp.where(causal_diag, diag, 0.0), scores)
    o = o_inter + _dot(scores.astype(BF16), v.astype(BF16))
    b_last = b[c - 1:c, :]
    ke = (kk * jnp.exp2(b_last - b)).astype(BF16)
    st_new = jnp.exp2(b_last) * st + _dot_tn(v.astype(BF16), ke)
    return o, st_new


def _hgrn_chunk_head_bounded(q, kk, v, b, st, causal):
    c = HGRN_CHUNK
    b_mid = b[c // 2 - 1:c // 2, :]
    qe = (q * jnp.exp2(b - b_mid)).astype(BF16)
    ke = (kk * jnp.exp2(b_mid - b)).astype(BF16)
    scores = jnp.where(causal, _dot_nt(qe, ke), 0.0)
    o = _dot_nt((q * jnp.exp2(b)).astype(BF16), st.astype(BF16)) + _dot(scores.astype(BF16), v.astype(BF16))
    b_last = b[c - 1:c, :]
    ks = (kk * jnp.exp2(b_last - b)).astype(BF16)
    st_new = jnp.exp2(b_last) * st + _dot_tn(v.astype(BF16), ks)
    return o, st_new


def _hgrn_masks():
    c = HGRN_CHUNK
    r = lax.broadcasted_iota(jnp.int32, (c, c), 0)
    s = lax.broadcasted_iota(jnp.int32, (c, c), 1)
    same_parent = {}
    n = HGRN_DIAG
    while n < c:
        shift = n.bit_length() - 1
        same_parent[n] = lax.shift_right_logical(r, shift) == lax.shift_right_logical(s, shift)
        n *= 2
    lane_mod = lax.broadcasted_iota(jnp.int32, (HGRN_DIAG, c), 1) & (HGRN_DIAG - 1)
    return r >= s, same_parent, lane_mod


def _mixer0_kernel(h_ref, g_ref, win_ref, poolw_ref, pscale_ref, lb_ref, onorm_ref, out_ref,
                   z_ref, upad_ref, state_ref, brow_ref, krow_ref, *, layer):
    t = h_ref.shape[1]
    si = pl.program_id(1)
    pool_w = len(POOL_WINDOWS) * POOL_GROUP_DIM
    n_heads = state_ref.shape[0]
    hw = n_heads * HGRN_HEAD_DIM

    @pl.when(si == 0)
    def _():
        state_ref[...] = jnp.zeros_like(state_ref)
        upad_ref[0:POOL_CARRY_ROWS, :] = jnp.zeros((POOL_CARRY_ROWS, pool_w), F32)

    hn = _rms(h_ref[0], g_ref[...]).astype(BF16)
    z_ref[...] = _dot(hn, win_ref[...])

    upad_ref[POOL_CARRY_ROWS:POOL_CARRY_ROWS + t, :] = z_ref[:, 0:pool_w]
    pos = lax.broadcasted_iota(jnp.int32, (t, 1), 0) + si * t
    for gi, w in enumerate(POOL_WINDOWS):
        cols = slice(gi * POOL_GROUP_DIM, (gi + 1) * POOL_GROUP_DIM)
        u = upad_ref[POOL_CARRY_ROWS:POOL_CARRY_ROWS + t, cols]
        acc = u
        for j in range(1, w):
            acc = acc + upad_ref[POOL_CARRY_ROWS - j:POOL_CARRY_ROWS - j + t, cols]
        cnt = jnp.minimum(pos + 1, w).astype(F32)
        dlt = acc / cnt - u
        y = _dot(dlt.astype(BF16), poolw_ref[gi]) * pscale_ref[:, cols]
        out_ref[0, :, cols] = y.astype(out_ref.dtype)
    upad_ref[0:POOL_CARRY_ROWS, :] = upad_ref[t:t + POOL_CARRY_ROWS, :]

    lbraw = lb_ref[...]
    lbe = jnp.exp(lbraw - jnp.max(lbraw, axis=0, keepdims=True))
    lbs = lbe / jnp.sum(lbe, axis=0, keepdims=True)
    lb = jnp.sum(lbs[0:layer + 1], axis=0, keepdims=True)
    c = HGRN_CHUNK
    ri = lax.broadcasted_iota(jnp.int32, (c, c), 0)
    ci = lax.broadcasted_iota(jnp.int32, (c, c), 1)
    ltri = jnp.where(ri >= ci, 1.0, 0.0).astype(BF16)
    onorm = onorm_ref[...]
    masks = _hgrn_masks()

    group = t // c
    head_cols = [slice(hd * HGRN_HEAD_DIM, (hd + 1) * HGRN_HEAD_DIM) for hd in range(n_heads)]

    zq, zi, zg, kk, b = [], [], [], [], []
    for u in range(group):
        rows = slice(u * c, (u + 1) * c)
        zq.append(z_ref[rows, pool_w:pool_w + hw])
        zf = z_ref[rows, pool_w + hw:pool_w + 2 * hw]
        zi.append(z_ref[rows, pool_w + 2 * hw:pool_w + 3 * hw])
        zg.append(z_ref[rows, pool_w + 3 * hw:pool_w + 4 * hw])
        f = lb + (1.0 - lb) * _sigmoid(zf)
        logf = jnp.log2(f)
        kk.append(1.0 - f)
        g_hi = logf.astype(BF16)
        r1 = logf - g_hi.astype(F32)
        g_mid = r1.astype(BF16)
        g_lo = (r1 - g_mid.astype(F32)).astype(BF16)
        b.append(_dot(ltri, g_hi) + _dot(ltri, g_mid) + _dot(ltri, g_lo))
    half_min = None
    for u in range(group):
        b_mid = b[u][c // 2 - 1:c // 2, :]
        worst = jnp.minimum(b_mid, b[u][c - 1:c, :] - b_mid)
        half_min = worst if half_min is None else jnp.minimum(half_min, worst)
    bounded = jnp.min(half_min) >= -HGRN_BOUNDED_LOG2_RANGE

    def run(chunk_head):
        states = [state_ref[hd] for hd in range(n_heads)]
        for u in range(group):
            for hd, cols in enumerate(head_cols):
                o, states[hd] = chunk_head(u, cols, states[hd])
                o = o * lax.rsqrt(jnp.mean(o * o, axis=-1, keepdims=True) + NORM_EPS) * onorm
                gate = zg[u][:, cols]
                o = o * (gate * _sigmoid(gate))
                out_ref[0, u * c:(u + 1) * c, pool_w + cols.start:pool_w + cols.stop] = o.astype(out_ref.dtype)
        for hd in range(n_heads):
            state_ref[hd] = states[hd]

    @pl.when(bounded)
    def _():
        run(lambda u, cols, st: _hgrn_chunk_head_bounded(
            zq[u][:, cols], kk[u][:, cols], zi[u][:, cols], b[u][:, cols], st, masks[0]))

    @pl.when(jnp.logical_not(bounded))
    def _():
        for u in range(group):
            brow_ref[u] = b[u]
            krow_ref[u] = kk[u]
        run(lambda u, cols, st: _hgrn_chunk_head(
            zq[u][:, cols], kk[u][:, cols], zi[u][:, cols], b[u][:, cols], st, masks,
            lambda i: brow_ref[u, i:i + 1, cols], lambda i: krow_ref[u, i:i + 1, cols]))


def _mixer0(h, g, w_in, pool_w, pool_scale, hgrn_lb, o_norm, layer):
    bsz, s, d = h.shape
    t = min(TOKEN_TILE, s)
    assert s % t == 0 and t % HGRN_CHUNK == 0
    zw = w_in.shape[1]
    pw = len(POOL_WINDOWS) * POOL_GROUP_DIM
    hw = hgrn_lb.shape[1]
    assert zw == pw + 4 * hw and hw % HGRN_HEAD_DIM == 0
    n_heads = hw // HGRN_HEAD_DIM
    return pl.pallas_call(
        functools.partial(_mixer0_kernel, layer=layer),
        grid=(bsz, s // t),
        in_specs=[
            pl.BlockSpec((1, t, d), lambda b, i: (b, i, 0)),
            _const_spec((1, d)),
            _const_spec((d, zw)),
            _const_spec(pool_w.shape),
            _const_spec((1, pw)),
            _const_spec(hgrn_lb.shape),
            _const_spec((1, HGRN_HEAD_DIM)),
        ],
        out_specs=pl.BlockSpec((1, t, pw + hw), lambda b, i: (b, i, 0)),
        out_shape=jax.ShapeDtypeStruct((bsz, s, pw + hw), BF16),
        scratch_shapes=[
            pltpu.VMEM((t, zw), F32),
            pltpu.VMEM((t + POOL_CARRY_ROWS, pw), F32),
            pltpu.VMEM((n_heads, HGRN_HEAD_DIM, HGRN_HEAD_DIM), F32),
            pltpu.VMEM((t // HGRN_CHUNK, HGRN_CHUNK, hw), F32),
            pltpu.VMEM((t // HGRN_CHUNK, HGRN_CHUNK, hw), F32),
        ],
        compiler_params=pltpu.CompilerParams(
            dimension_semantics=("arbitrary", "arbitrary"), vmem_limit_bytes=VMEM_LIMIT_BYTES),
        name="mixer0",
    )(h, g.reshape(1, d), w_in.astype(BF16), pool_w.astype(BF16), pool_scale.reshape(1, pw),
      hgrn_lb, o_norm.reshape(1, HGRN_HEAD_DIM))


def _post_kernel(h_ref, mix_ref, p_ref, wmix_ref, gm_ref, wup_ref, wdn_ref, gp_ref, wg_ref, wple_ref, out_ref,
                 *, ff_chunk):
    h1 = h_ref[0] + _dot(mix_ref[0], wmix_ref[...])
    hm = _rms(h1, gm_ref[...]).astype(BF16)
    d_ff = wup_ref.shape[1]
    acc = jnp.zeros_like(h1)
    for c0 in range(0, d_ff, ff_chunk):
        up = _dot(hm, wup_ref[:, c0:c0 + ff_chunk])
        act = jnp.square(jnp.maximum(up, 0.0)).astype(BF16)
        acc = acc + _dot(act, wdn_ref[c0:c0 + ff_chunk, :])
    h2 = h1 + acc
    gate = _sigmoid(_dot(_rms(h2, gp_ref[...]).astype(BF16), wg_ref[...]))
    pe = _dot(p_ref[0, 0].astype(BF16), wple_ref[...])
    out_ref[0] = h2 + pe * gate


def _post(h, mix, p, layer, w_mix, g_mlp, w_up, w_down, g_ple, w_gate, w_ple):
    bsz, s, d = h.shape
    t = min(TOKEN_TILE, s)
    assert s % t == 0
    d_mix = mix.shape[-1]
    d_ff = w_up.shape[1]
    d_p = p.shape[-1]
    ff_chunk = min(1024, d_ff)
    assert d_ff % ff_chunk == 0
    tok = lambda width: pl.BlockSpec((1, t, width), lambda b, i: (b, i, 0))
    return pl.pallas_call(
        functools.partial(_post_kernel, ff_chunk=ff_chunk),
        grid=(bsz, s // t),
        in_specs=[
            tok(d), tok(d_mix),
            pl.BlockSpec((1, 1, t, d_p), lambda b, i: (layer, b, i, 0)),
            _const_spec((d_mix, d)),
            _const_spec((1, d)),
            _const_spec((d, d_ff)),
            _const_spec((d_ff, d)),
            _const_spec((1, d)),
            _const_spec((d, d)),
            _const_spec((d_p, d)),
        ],
        out_specs=tok(d),
        out_shape=jax.ShapeDtypeStruct((bsz, s, d), F32),
        compiler_params=pltpu.CompilerParams(
            dimension_semantics=("arbitrary", "arbitrary"), vmem_limit_bytes=VMEM_LIMIT_BYTES),
        name="post",
    )(h, mix, p, w_mix.astype(BF16), g_mlp.reshape(1, d), w_up.astype(BF16), w_down.astype(BF16),
      g_ple.reshape(1, d), w_gate.astype(BF16), w_ple.astype(BF16))


def _rope_kernel(inv_ref, cos_ref, sin_ref):
    t = cos_ref.shape[0]
    pos = (lax.broadcasted_iota(jnp.int32, (t, LANES), 0) + pl.program_id(0) * t).astype(F32)
    lane = lax.broadcasted_iota(jnp.int32, (t, LANES), 1)
    ang = pos * inv_ref[...]
    cos_ref[...] = jnp.cos(ang)
    sin = jnp.sin(ang)
    sin_ref[...] = jnp.where(lane < LANES // 2, -sin, sin)


def _rope_tables(s):
    half = ATTN_HEAD_DIM // 2
    inv = ROPE_THETA ** (-jnp.arange(half, dtype=F32) / half)
    inv = jnp.tile(inv, LANES // half).reshape(1, LANES)
    t = min(1024, s)
    assert s % t == 0
    return pl.pallas_call(
        _rope_kernel,
        grid=(s // t,),
        in_specs=[pl.BlockSpec((1, LANES), lambda i: (0, 0))],
        out_specs=[pl.BlockSpec((t, LANES), lambda i: (i, 0))] * 2,
        out_shape=[jax.ShapeDtypeStruct((s, LANES), F32)] * 2,
        name="rope_tables",
    )(inv)


def _rope_lane_perm(d):
    half = ATTN_HEAD_DIM // 2
    lane = jnp.arange(LANES)
    col_in_pair = ((lane // half) % 2) * ATTN_HEAD_DIM + (lane // (2 * half)) * half + lane % half
    return (jnp.arange(d // LANES)[:, None] * LANES + col_in_pair[None, :]).reshape(-1)


def _qkv_kernel(h_ref, g_ref, w_ref, qg_ref, kg_ref, cos_ref, sin_ref, same_head_ref, *refs):
    out_refs, (z_ref, stage_ref, tmp_ref) = refs[:-3], refs[-3:]
    t, d = h_ref.shape[1], h_ref.shape[2]

    @pl.when(pl.program_id(0) == 0)
    def _():
        z_ref[...] = jnp.zeros_like(z_ref)
    d1, d2 = DILATIONS[1], DILATIONS[2]
    n1, n2 = t // d1, t // d2
    group = same_head_ref.shape[0]
    hn = _rms(h_ref[0], g_ref[...]).astype(BF16)
    cos = cos_ref[...]
    sin = sin_ref[...]
    emitted = [0]

    def emit(xb, outs, hp):
        nat_ref, l1_ref, l2_ref = outs
        slot = emitted[0] % stage_ref.shape[0]
        emitted[0] += 1
        nat_ref[0, hp] = xb.astype(BF16)
        stage_ref[slot] = xb
        for r in range(d1):
            y = stage_ref[slot, pl.ds(r, n1, stride=d1), :]
            l1_ref[0, hp, r] = y.astype(BF16)
            tmp_ref[slot, r * n1:(r + 1) * n1, :] = y
        for r in range(d2):
            y = tmp_ref[slot, pl.ds((r % d1) * n1 + r // d1, n2, stride=d2 // d1), :]
            l2_ref[0, hp, r] = y.astype(BF16)

    for c0 in range(0, 3 * d, group):
        zc = z_ref[:, c0:c0 + group]
        z_ref[:, c0:c0 + group] = _dot(hn, w_ref[:, c0:c0 + group])
        which, col = divmod(c0, d)
        outs = out_refs[3 * which:3 * which + 3]
        if which < 2:
            ss = _dot((zc * zc).astype(BF16), same_head_ref[...])
            gain = (qg_ref, kg_ref)[which][:, col:col + group]
            zc = zc * lax.rsqrt(ss * (1.0 / ATTN_HEAD_DIM) + NORM_EPS) * gain
        for j in range(group // LANES):
            xb = zc[:, j * LANES:(j + 1) * LANES]
            if which < 2:
                xb = xb * cos + pltpu.roll(xb, LANES // 2, 1) * sin
            emit(xb, outs, (col + j * LANES) // LANES)


def _qkv(h, g, w_qkv, q_gain, k_gain, cos, sin):
    bsz, s, d = h.shape
    t = min(TOKEN_TILE, s)
    assert s % t == 0 and w_qkv.shape[1] == 3 * d and d % LANES == 0
    n_pairs = d // LANES
    d1, d2 = DILATIONS[1], DILATIONS[2]
    assert t % (d2 * 16) == 0
    half = ATTN_HEAD_DIM // 2

    def rope_order(cols):
        lead = cols.shape[:-1]
        x = cols.reshape(lead + (n_pairs, 2, 2, half))
        return jnp.swapaxes(x, -3, -2).reshape(lead + (d,))

    w = jnp.concatenate([rope_order(w_qkv[:, :d]), rope_order(w_qkv[:, d:2 * d]), w_qkv[:, 2 * d:]], axis=1)
    q_gain_lanes = rope_order(jnp.tile(q_gain * (ATTN_HEAD_DIM ** -0.5 * LOG2_E), d // ATTN_HEAD_DIM))
    k_gain_lanes = rope_order(jnp.tile(k_gain, d // ATTN_HEAD_DIM))
    group = 2 * LANES
    head_of_lane = _rope_lane_perm(d)[:group] // ATTN_HEAD_DIM
    same_head = (head_of_lane[:, None] == head_of_lane[None, :]).astype(BF16)
    shapes = [(bsz, n_pairs, s, LANES), (bsz, n_pairs, d1, s // d1, LANES), (bsz, n_pairs, d2, s // d2, LANES)]
    n_seq = s // t
    n_tiles = bsz * n_seq
    proj = lambda j: jnp.minimum(j, n_tiles - 1)
    done = lambda j: jnp.maximum(j - 1, 0)
    specs = [
        pl.BlockSpec((1, n_pairs, t, LANES), lambda j: (done(j) // n_seq, 0, done(j) % n_seq, 0)),
        pl.BlockSpec((1, n_pairs, d1, t // d1, LANES), lambda j: (done(j) // n_seq, 0, 0, done(j) % n_seq, 0)),
        pl.BlockSpec((1, n_pairs, d2, t // d2, LANES), lambda j: (done(j) // n_seq, 0, 0, done(j) % n_seq, 0)),
    ]
    outs = pl.pallas_call(
        _qkv_kernel,
        grid=(n_tiles + 1,),
        in_specs=[
            pl.BlockSpec((1, t, d), lambda j: (proj(j) // n_seq, proj(j) % n_seq, 0)),
            _const_spec((1, d)),
            _const_spec((d, 3 * d)),
            _const_spec((1, d)),
            _const_spec((1, d)),
            pl.BlockSpec((t, LANES), lambda j: (done(j) % n_seq, 0)),
            pl.BlockSpec((t, LANES), lambda j: (done(j) % n_seq, 0)),
            _const_spec((group, group)),
        ],
        out_specs=specs * 3,
        out_shape=[jax.ShapeDtypeStruct(sh, BF16) for sh in shapes] * 3,
        scratch_shapes=[pltpu.VMEM((t, 3 * d), F32), pltpu.VMEM((2, t, LANES), F32),
                        pltpu.VMEM((2, t, LANES), F32)],
        compiler_params=pltpu.CompilerParams(
            dimension_semantics=("arbitrary",), vmem_limit_bytes=VMEM_LIMIT_BYTES),
        name="qkv",
    )(h, g.reshape(1, d), w.astype(BF16), q_gain_lanes.reshape(1, d), k_gain_lanes.reshape(1, d), cos, sin,
      same_head)
    return outs[0:3], outs[3:6], outs[6:9]


def _attn_kernel(q1_ref, q4_ref, q16_ref, k1_ref, k4_ref, k16_ref, v1_ref, v4_ref, v16_ref, out_ref,
                 bias_ref, acc_ref, m_ref, l_ref, merged_ref):
    tq = q1_ref.shape[2]
    bq = ATTN_BLOCK_Q
    w = ATTN_WINDOW_STEPS
    nk = bq + w
    t0 = pl.program_id(2) * tq
    d1 = DILATIONS[1]

    def rows_of(ref, res, start, size):
        if len(ref.shape) == 4:
            return ref[0, 0, pl.ds(start, size), :]
        return ref[0, 0, res, pl.ds(start, size), :]

    i_minus_j = (lax.broadcasted_iota(jnp.int32, (bq, nk), 0) - lax.broadcasted_iota(jnp.int32, (bq, nk), 1))
    for variant, koff in enumerate((w, 0)):
        dist = i_minus_j + koff
        bias_ref[variant] = jnp.where((dist >= 0) & (dist <= w), 0.0, MASK_VALUE)

    lane = lax.broadcasted_iota(jnp.int32, (bq, LANES), 1)
    head0 = lane < ATTN_HEAD_DIM
    q_head0 = jnp.where((lane // (ATTN_HEAD_DIM // 2)) % 2 == 0, 1.0, 0.0).astype(BF16)
    q_head1 = (1.0 - q_head0.astype(F32)).astype(BF16)
    ones = jnp.ones((nk, LANES), BF16)

    branches = list(enumerate(
        zip(DILATIONS, (q1_ref, q4_ref, q16_ref), (k1_ref, k4_ref, k16_ref), (v1_ref, v4_ref, v16_ref))))
    n1 = tq // d1
    rows_per_step = ATTN_MERGE_ROWS
    for br, (dil, q_ref, k_ref, v_ref) in branches[2:] + branches[:1] + branches[1:2]:
        blocks_per_res = tq // (dil * bq)
        for res in range(dil):
            for blk in range(blocks_per_res):
                l_k0 = t0 // dil + bq * blk - w
                variant = (l_k0 < 0).astype(jnp.int32)
                row0 = pl.multiple_of(jnp.maximum(l_k0, 0), bq)
                qb = rows_of(q_ref, res, blk * bq, bq)
                kb = rows_of(k_ref, res, row0, nk)
                vb = rows_of(v_ref, res, row0, nk)
                qq = jnp.concatenate([qb * q_head0, qb * q_head1], axis=0)
                s = _dot_nt(qq, kb)
                bias = bias_ref[variant]
                s = jnp.concatenate([s[:bq] + bias, s[bq:] + bias], axis=0)
                m = jnp.max(s, axis=-1, keepdims=True)
                p = jnp.exp2(s - m).astype(BF16)
                pv = _dot(p, jnp.concatenate([vb, ones], axis=1))
                acc = jnp.where(head0, pv[:bq, :LANES], pv[bq:, :LANES])
                l = jnp.where(head0, pv[:bq, LANES:], pv[bq:, LANES:])
                m = jnp.where(head0, m[:bq], m[bq:])
                if dil == 1:
                    rows = pl.ds(blk * bq, bq)
                elif dil != d1:
                    rows = pl.ds((res % d1) * n1 + res // d1 + (dil // d1) * bq * blk, bq, stride=dil // d1)
                if dil != d1:
                    acc_ref[br, rows, :] = acc
                    l_ref[br, rows, :] = l
                    m_ref[br, rows, :] = m
                    continue
                for c0 in range(0, bq, rows_per_step):
                    tok = pl.ds(res + d1 * (blk * bq + c0), rows_per_step, stride=d1)
                    res_rows = pl.ds(res * n1 + blk * bq + c0, rows_per_step)
                    sl = slice(c0, c0 + rows_per_step)
                    parts = [(m[sl], l[sl], acc[sl])]
                    for other in range(len(DILATIONS)):
                        if other != br:
                            src = tok if DILATIONS[other] == 1 else res_rows
                            parts.append((m_ref[other, src, :], l_ref[other, src, :], acc_ref[other, src, :]))
                    m_all = functools.reduce(jnp.maximum, [part[0] for part in parts])
                    num = jnp.zeros((rows_per_step, LANES), F32)
                    den = jnp.zeros((rows_per_step, LANES), F32)
                    for m_br, l_br, acc_br in parts:
                        wgt = jnp.exp2(m_br - m_all)
                        num = num + wgt * acc_br
                        den = den + wgt * l_br
                    merged_ref[tok, :] = num / den

    out_ref[0] = merged_ref[...].astype(out_ref.dtype)


def _attention(q, k, v):
    bsz, n_pairs, s, _ = q[0].shape
    tq = ATTN_Q_TILE
    assert s % tq == 0 and tq == ATTN_BLOCK_Q * max(DILATIONS) and s >= 2 * tq
    assert len(DILATIONS) == 3 and DILATIONS[0] == 1
    nb = len(DILATIONS)

    def spec(dil, rows, whole):
        if dil == 1:
            return pl.BlockSpec((1, 1, rows, LANES), lambda b, hp, i: (b, hp, 0 if whole else i, 0))
        return pl.BlockSpec((1, 1, dil, rows, LANES), lambda b, hp, i: (b, hp, 0, 0 if whole else i, 0))

    q_specs = [spec(dil, tq // dil, False) for dil in DILATIONS]
    kv_specs = [spec(dil, s // dil, True) for dil in DILATIONS]
    return pl.pallas_call(
        _attn_kernel,
        grid=(bsz, n_pairs, s // tq),
        in_specs=q_specs + kv_specs + kv_specs,
        out_specs=pl.BlockSpec((1, tq, LANES), lambda b, hp, i: (b, i, hp)),
        out_shape=jax.ShapeDtypeStruct((bsz, s, n_pairs * LANES), BF16),
        scratch_shapes=[
            pltpu.VMEM((2, ATTN_BLOCK_Q, ATTN_BLOCK_Q + ATTN_WINDOW_STEPS), F32),
        ] + [pltpu.VMEM((nb, tq, LANES), F32)] * 3 + [pltpu.VMEM((tq, LANES), F32)],
        compiler_params=pltpu.CompilerParams(
            dimension_semantics=("arbitrary", "arbitrary", "arbitrary"), vmem_limit_bytes=VMEM_LIMIT_BYTES),
        name="dilated_attention",
    )(*q, *k, *v)


def kernel(x, p, mix_norm, w_in_ab, pool_w, pool_scale, hgrn_lb, hgrn_o_norm, w_out_ab, w_qkv, q_norm, k_norm,
           w_o, mlp_norm, w_up, w_down, ple_norm, w_ple, w_ple_gate):
    depth = p.shape[0]
    s = x.shape[1]
    h = x
    cos = sin = None
    for layer in range(depth):
        if layer % 2 == 0:
            e = layer // 2
            mix = _mixer0(h, mix_norm[layer], w_in_ab[e], pool_w[e], pool_scale[e], hgrn_lb, hgrn_o_norm[e], layer)
            w_mix = w_out_ab[e]
        else:
            ci = layer // 2
            if cos is None:
                cos, sin = _rope_tables(s)
            q, k, v = _qkv(h, mix_norm[layer], w_qkv[ci], q_norm[ci], k_norm[ci], cos, sin)
            mix = _attention(q, k, v)
            w_mix = w_o[ci]
        h = _post(h, mix, p, layer, w_mix, mlp_norm[layer], w_up[layer], w_down[layer], ple_norm[layer],
                  w_ple_gate[layer], w_ple[layer])
    return h
```

```python
import functools

import jax
import jax.numpy as jnp
from jax import lax
from jax.experimental import pallas as pl
from jax.experimental.pallas import tpu as pltpu

F32 = jnp.float32
BF16 = jnp.bfloat16

NORM_EPS = 1e-6
POOL_WINDOWS = (2, 4, 8, 16)
POOL_GROUP_DIM = 128
POOL_CARRY_ROWS = 16
HGRN_HEAD_DIM = 128
HGRN_CHUNK = 128
HGRN_DIAG = 8
HGRN_BOUNDED_LOG2_RANGE = 120.0
ATTN_HEAD_DIM = 64
ATTN_WINDOW_STEPS = 128
DILATIONS = (1, 4, 16)
ATTN_BLOCK_Q = 128
ROPE_THETA = 10000.0
LANES = 128
MASK_VALUE = -1e30
LOG2_E = 1.4426950408889634

VMEM_LIMIT_BYTES = 56 * 1024 * 1024

TOKEN_TILE = 512
ATTN_Q_TILE = 4096
ATTN_MERGE_ROWS = 32


def _dot(a, b):
    return jnp.dot(a, b, preferred_element_type=F32)


def _dot_nt(a, b):
    return lax.dot_general(a, b, (((1,), (1,)), ((), ())), preferred_element_type=F32)


def _dot_tn(a, b):
    return lax.dot_general(a, b, (((0,), (0,)), ((), ())), preferred_element_type=F32)


def _rms(x, g):
    ms = jnp.mean(x * x, axis=-1, keepdims=True)
    return x * lax.rsqrt(ms + NORM_EPS) * g


def _sigmoid(x):
    return 1.0 / (1.0 + jnp.exp(-x))


def _const_spec(shape):
    nd = len(shape)
    return pl.BlockSpec(shape, lambda *_: (0,) * nd, pipeline_mode=pl.Buffered(1))


def _hgrn_diag_scores(qb, bb, b_row, k_row, lane_mod):
    n = qb.shape[0]
    scores = jnp.zeros((n, HGRN_CHUNK), F32)
    for s in range(n):
        e = jnp.exp2(bb - b_row(s))
        a = jnp.sum(qb * e * k_row(s), axis=-1, keepdims=True)
        scores = jnp.where(lane_mod == s, a, scores)
    return scores


def _hgrn_level_scores(q, kk, b, span):
    c = q.shape[0]
    zeros = jnp.zeros((span, q.shape[1]), F32)
    q_parts, k_parts = [], []
    for start in range(0, c, 2 * span):
        mid = start + span
        anchor = b[mid - 1:mid, :]
        lo = slice(start, mid)
        hi = slice(mid, mid + span)
        q_parts += [zeros, q[hi] * jnp.exp2(b[hi] - anchor)]
        k_parts += [kk[lo] * jnp.exp2(anchor - b[lo]), zeros]
    qe = jnp.concatenate(q_parts, axis=0).astype(BF16)
    ke = jnp.concatenate(k_parts, axis=0).astype(BF16)
    return _dot_nt(qe, ke)


def _hgrn_chunk_head(q, kk, v, b, st, masks, b_row, k_row):
    c = HGRN_CHUNK
    d = HGRN_DIAG
    causal_diag, same_parent, lane_mod = masks
    o_inter = _dot_nt((q * jnp.exp2(b)).astype(BF16), st.astype(BF16))
    diag = jnp.concatenate(
        [_hgrn_diag_scores(q[i:i + d], b[i:i + d], lambda s, i=i: b_row(i + s), lambda s, i=i: k_row(i + s),
                           lane_mod) for i in range(0, c, d)], axis=0)
    span = c // 2
    scores = _hgrn_level_scores(q, kk, b, span)
    while span > d:
        span //= 2
        scores = jnp.where(same_parent[2 * span], _hgrn_level_scores(q, kk, b, span), scores)
    scores = jnp.where(same_parent[d], jnp.where(causal_diag, diag, 0.0), scores)
    o = o_inter + _dot(scores.astype(BF16), v.astype(BF16))
    b_last = b[c - 1:c, :]
    ke = (kk * jnp.exp2(b_last - b)).astype(BF16)
    st_new = jnp.exp2(b_last) * st + _dot_tn(v.astype(BF16), ke)
    return o, st_new


def _hgrn_chunk_head_bounded(q, kk, v, b, st, causal):
    c = HGRN_CHUNK
    b_mid = b[c // 2 - 1:c // 2, :]
    qe = (q * jnp.exp2(b - b_mid)).astype(BF16)
    ke = (kk * jnp.exp2(b_mid - b)).astype(BF16)
    scores = jnp.where(causal, _dot_nt(qe, ke), 0.0)
    o = _dot_nt((q * jnp.exp2(b)).astype(BF16), st.astype(BF16)) + _dot(scores.astype(BF16), v.astype(BF16))
    b_last = b[c - 1:c, :]
    ks = (kk * jnp.exp2(b_last - b)).astype(BF16)
    st_new = jnp.exp2(b_last) * st + _dot_tn(v.astype(BF16), ks)
    return o, st_new


def _hgrn_masks():
    c = HGRN_CHUNK
    r = lax.broadcasted_iota(jnp.int32, (c, c), 0)
    s = lax.broadcasted_iota(jnp.int32, (c, c), 1)
    same_parent = {}
    n = HGRN_DIAG
    while n < c:
        shift = n.bit_length() - 1
        same_parent[n] = lax.shift_right_logical(r, shift) == lax.shift_right_logical(s, shift)
        n *= 2
    lane_mod = lax.broadcasted_iota(jnp.int32, (HGRN_DIAG, c), 1) & (HGRN_DIAG - 1)
    return r >= s, same_parent, lane_mod


def _mixer0_kernel(h_ref, g_ref, win_ref, poolw_ref, pscale_ref, lb_ref, onorm_ref, out_ref,
                   z_ref, upad_ref, state_ref, brow_ref, krow_ref, *, layer):
    t = h_ref.shape[1]
    si = pl.program_id(1)
    pool_w = len(POOL_WINDOWS) * POOL_GROUP_DIM
    n_heads = state_ref.shape[0]
    hw = n_heads * HGRN_HEAD_DIM

    @pl.when(si == 0)
    def _():
        state_ref[...] = jnp.zeros_like(state_ref)
        upad_ref[0:POOL_CARRY_ROWS, :] = jnp.zeros((POOL_CARRY_ROWS, pool_w), F32)

    hn = _rms(h_ref[0], g_ref[...]).astype(BF16)
    z_ref[...] = _dot(hn, win_ref[...])

    upad_ref[POOL_CARRY_ROWS:POOL_CARRY_ROWS + t, :] = z_ref[:, 0:pool_w]
    pos = lax.broadcasted_iota(jnp.int32, (t, 1), 0) + si * t
    for gi, w in enumerate(POOL_WINDOWS):
        cols = slice(gi * POOL_GROUP_DIM, (gi + 1) * POOL_GROUP_DIM)
        u = upad_ref[POOL_CARRY_ROWS:POOL_CARRY_ROWS + t, cols]
        acc = u
        for j in range(1, w):
            acc = acc + upad_ref[POOL_CARRY_ROWS - j:POOL_CARRY_ROWS - j + t, cols]
        cnt = jnp.minimum(pos + 1, w).astype(F32)
        dlt = acc / cnt - u
        y = _dot(dlt.astype(BF16), poolw_ref[gi]) * pscale_ref[:, cols]
        out_ref[0, :, cols] = y.astype(out_ref.dtype)
    upad_ref[0:POOL_CARRY_ROWS, :] = upad_ref[t:t + POOL_CARRY_ROWS, :]

    lbraw = lb_ref[...]
    lbe = jnp.exp(lbraw - jnp.max(lbraw, axis=0, keepdims=True))
    lbs = lbe / jnp.sum(lbe, axis=0, keepdims=True)
    lb = jnp.sum(lbs[0:layer + 1], axis=0, keepdims=True)
    c = HGRN_CHUNK
    ri = lax.broadcasted_iota(jnp.int32, (c, c), 0)
    ci = lax.broadcasted_iota(jnp.int32, (c, c), 1)
    ltri = jnp.where(ri >= ci, 1.0, 0.0).astype(BF16)
    onorm = onorm_ref[...]
    masks = _hgrn_masks()

    group = t // c
    head_cols = [slice(hd * HGRN_HEAD_DIM, (hd + 1) * HGRN_HEAD_DIM) for hd in range(n_heads)]

    zq, zi, zg, kk, b = [], [], [], [], []
    for u in range(group):
        rows = slice(u * c, (u + 1) * c)
        zq.append(z_ref[rows, pool_w:pool_w + hw])
        zf = z_ref[rows, pool_w + hw:pool_w + 2 * hw]
        zi.append(z_ref[rows, pool_w + 2 * hw:pool_w + 3 * hw])
        zg.append(z_ref[rows, pool_w + 3 * hw:pool_w + 4 * hw])
        f = lb + (1.0 - lb) * _sigmoid(zf)
        logf = jnp.log2(f)
        kk.append(1.0 - f)
        g_hi = logf.astype(BF16)
        r1 = logf - g_hi.astype(F32)
        g_mid = r1.astype(BF16)
        g_lo = (r1 - g_mid.astype(F32)).astype(BF16)
        b.append(_dot(ltri, g_hi) + _dot(ltri, g_mid) + _dot(ltri, g_lo))
    half_min = None
    for u in range(group):
        b_mid = b[u][c // 2 - 1:c // 2, :]
        worst = jnp.minimum(b_mid, b[u][c - 1:c, :] - b_mid)
        half_min = worst if half_min is None else jnp.minimum(half_min, worst)
    bounded = jnp.min(half_min) >= -HGRN_BOUNDED_LOG2_RANGE

    def run(chunk_head):
        states = [state_ref[hd] for hd in range(n_heads)]
        for u in range(group):
            for hd, cols in enumerate(head_cols):
                o, states[hd] = chunk_head(u, cols, states[hd])
                o = o * lax.rsqrt(jnp.mean(o * o, axis=-1, keepdims=True) + NORM_EPS) * onorm
                gate = zg[u][:, cols]
                o = o * (gate * _sigmoid(gate))
                out_ref[0, u * c:(u + 1) * c, pool_w + cols.start:pool_w + cols.stop] = o.astype(out_ref.dtype)
        for hd in range(n_heads):
            state_ref[hd] = states[hd]

    @pl.when(bounded)
    def _():
        run(lambda u, cols, st: _hgrn_chunk_head_bounded(
            zq[u][:, cols], kk[u][:, cols], zi[u][:, cols], b[u][:, cols], st, masks[0]))

    @pl.when(jnp.logical_not(bounded))
    def _():
        for u in range(group):
            brow_ref[u] = b[u]
            krow_ref[u] = kk[u]
        run(lambda u, cols, st: _hgrn_chunk_head(
            zq[u][:, cols], kk[u][:, cols], zi[u][:, cols], b[u][:, cols], st, masks,
            lambda i: brow_ref[u, i:i + 1, cols], lambda i: krow_ref[u, i:i + 1, cols]))


def _mixer0(h, g, w_in, pool_w, pool_scale, hgrn_lb, o_norm, layer):
    bsz, s, d = h.shape
    t = min(TOKEN_TILE, s)
    assert s % t == 0 and t % HGRN_CHUNK == 0
    zw = w_in.shape[1]
    pw = len(POOL_WINDOWS) * POOL_GROUP_DIM
    hw = hgrn_lb.shape[1]
    assert zw == pw + 4 * hw and hw % HGRN_HEAD_DIM == 0
    n_heads = hw // HGRN_HEAD_DIM
    return pl.pallas_call(
        functools.partial(_mixer0_kernel, layer=layer),
        grid=(bsz, s // t),
        in_specs=[
            pl.BlockSpec((1, t, d), lambda b, i: (b, i, 0)),
            _const_spec((1, d)),
            _const_spec((d, zw)),
            _const_spec(pool_w.shape),
            _const_spec((1, pw)),
            _const_spec(hgrn_lb.shape),
            _const_spec((1, HGRN_HEAD_DIM)),
        ],
        out_specs=pl.BlockSpec((1, t, pw + hw), lambda b, i: (b, i, 0)),
        out_shape=jax.ShapeDtypeStruct((bsz, s, pw + hw), BF16),
        scratch_shapes=[
            pltpu.VMEM((t, zw), F32),
            pltpu.VMEM((t + POOL_CARRY_ROWS, pw), F32),
            pltpu.VMEM((n_heads, HGRN_HEAD_DIM, HGRN_HEAD_DIM), F32),
            pltpu.VMEM((t // HGRN_CHUNK, HGRN_CHUNK, hw), F32),
            pltpu.VMEM((t // HGRN_CHUNK, HGRN_CHUNK, hw), F32),
        ],
        compiler_params=pltpu.CompilerParams(
            dimension_semantics=("arbitrary", "arbitrary"), vmem_limit_bytes=VMEM_LIMIT_BYTES),
        name="mixer0",
    )(h, g.reshape(1, d), w_in.astype(BF16), pool_w.astype(BF16), pool_scale.reshape(1, pw),
      hgrn_lb, o_norm.reshape(1, HGRN_HEAD_DIM))


def _post_kernel(h_ref, mix_ref, p_ref, wmix_ref, gm_ref, wup_ref, wdn_ref, gp_ref, wg_ref, wple_ref, out_ref,
                 *, ff_chunk):
    h1 = h_ref[0] + _dot(mix_ref[0], wmix_ref[...])
    hm = _rms(h1, gm_ref[...]).astype(BF16)
    d_ff = wup_ref.shape[1]
    acc = jnp.zeros_like(h1)
    for c0 in range(0, d_ff, ff_chunk):
        up = _dot(hm, wup_ref[:, c0:c0 + ff_chunk])
        act = jnp.square(jnp.maximum(up, 0.0)).astype(BF16)
        acc = acc + _dot(act, wdn_ref[c0:c0 + ff_chunk, :])
    h2 = h1 + acc
    gate = _sigmoid(_dot(_rms(h2, gp_ref[...]).astype(BF16), wg_ref[...]))
    pe = _dot(p_ref[0, 0].astype(BF16), wple_ref[...])
    out_ref[0] = h2 + pe * gate


def _post(h, mix, p, layer, w_mix, g_mlp, w_up, w_down, g_ple, w_gate, w_ple):
    bsz, s, d = h.shape
    t = min(TOKEN_TILE, s)
    assert s % t == 0
    d_mix = mix.shape[-1]
    d_ff = w_up.shape[1]
    d_p = p.shape[-1]
    ff_chunk = min(1024, d_ff)
    assert d_ff % ff_chunk == 0
    tok = lambda width: pl.BlockSpec((1, t, width), lambda b, i: (b, i, 0))
    return pl.pallas_call(
        functools.partial(_post_kernel, ff_chunk=ff_chunk),
        grid=(bsz, s // t),
        in_specs=[
            tok(d), tok(d_mix),
            pl.BlockSpec((1, 1, t, d_p), lambda b, i: (layer, b, i, 0)),
            _const_spec((d_mix, d)),
            _const_spec((1, d)),
            _const_spec((d, d_ff)),
            _const_spec((d_ff, d)),
            _const_spec((1, d)),
            _const_spec((d, d)),
            _const_spec((d_p, d)),
        ],
        out_specs=tok(d),
        out_shape=jax.ShapeDtypeStruct((bsz, s, d), F32),
        compiler_params=pltpu.CompilerParams(
            dimension_semantics=("arbitrary", "arbitrary"), vmem_limit_bytes=VMEM_LIMIT_BYTES),
        name="post",
    )(h, mix, p, w_mix.astype(BF16), g_mlp.reshape(1, d), w_up.astype(BF16), w_down.astype(BF16),
      g_ple.reshape(1, d), w_gate.astype(BF16), w_ple.astype(BF16))


def _rope_kernel(inv_ref, cos_ref, sin_ref):
    t = cos_ref.shape[0]
    pos = (lax.broadcasted_iota(jnp.int32, (t, LANES), 0) + pl.program_id(0) * t).astype(F32)
    lane = lax.broadcasted_iota(jnp.int32, (t, LANES), 1)
    ang = pos * inv_ref[...]
    cos_ref[...] = jnp.cos(ang)
    sin = jnp.sin(ang)
    sin_ref[...] = jnp.where(lane < LANES // 2, -sin, sin)


def _rope_tables(s):
    half = ATTN_HEAD_DIM // 2
    inv = ROPE_THETA ** (-jnp.arange(half, dtype=F32) / half)
    inv = jnp.tile(inv, LANES // half).reshape(1, LANES)
    t = min(1024, s)
    assert s % t == 0
    return pl.pallas_call(
        _rope_kernel,
        grid=(s // t,),
        in_specs=[pl.BlockSpec((1, LANES), lambda i: (0, 0))],
        out_specs=[pl.BlockSpec((t, LANES), lambda i: (i, 0))] * 2,
        out_shape=[jax.ShapeDtypeStruct((s, LANES), F32)] * 2,
        name="rope_tables",
    )(inv)


def _rope_lane_perm(d):
    half = ATTN_HEAD_DIM // 2
    lane = jnp.arange(LANES)
    col_in_pair = ((lane // half) % 2) * ATTN_HEAD_DIM + (lane // (2 * half)) * half + lane % half
    return (jnp.arange(d // LANES)[:, None] * LANES + col_in_pair[None, :]).reshape(-1)


def _qkv_kernel(h_ref, g_ref, w_ref, qg_ref, kg_ref, cos_ref, sin_ref, same_head_ref, *refs):
    out_refs, (z_ref, stage_ref, tmp_ref) = refs[:-3], refs[-3:]
    t, d = h_ref.shape[1], h_ref.shape[2]

    @pl.when(pl.program_id(0) == 0)
    def _():
        z_ref[...] = jnp.zeros_like(z_ref)
    d1, d2 = DILATIONS[1], DILATIONS[2]
    n1, n2 = t // d1, t // d2
    group = same_head_ref.shape[0]
    hn = _rms(h_ref[0], g_ref[...]).astype(BF16)
    cos = cos_ref[...]
    sin = sin_ref[...]
    emitted = [0]

    def emit(xb, outs, hp):
        nat_ref, l1_ref, l2_ref = outs
        slot = emitted[0] % stage_ref.shape[0]
        emitted[0] += 1
        nat_ref[0, hp] = xb.astype(BF16)
        stage_ref[slot] = xb
        for r in range(d1):
            y = stage_ref[slot, pl.ds(r, n1, stride=d1), :]
            l1_ref[0, hp, r] = y.astype(BF16)
            tmp_ref[slot, r * n1:(r + 1) * n1, :] = y
        for r in range(d2):
            y = tmp_ref[slot, pl.ds((r % d1) * n1 + r // d1, n2, stride=d2 // d1), :]
            l2_ref[0, hp, r] = y.astype(BF16)

    for c0 in range(0, 3 * d, group):
        zc = z_ref[:, c0:c0 + group]
        z_ref[:, c0:c0 + group] = _dot(hn, w_ref[:, c0:c0 + group])
        which, col = divmod(c0, d)
        outs = out_refs[3 * which:3 * which + 3]
        if which < 2:
            ss = _dot((zc * zc).astype(BF16), same_head_ref[...])
            gain = (qg_ref, kg_ref)[which][:, col:col + group]
            zc = zc * lax.rsqrt(ss * (1.0 / ATTN_HEAD_DIM) + NORM_EPS) * gain
        for j in range(group // LANES):
            xb = zc[:, j * LANES:(j + 1) * LANES]
            if which < 2:
                xb = xb * cos + pltpu.roll(xb, LANES // 2, 1) * sin
            emit(xb, outs, (col + j * LANES) // LANES)


def _qkv(h, g, w_qkv, q_gain, k_gain, cos, sin):
    bsz, s, d = h.shape
    t = min(TOKEN_TILE, s)
    assert s % t == 0 and w_qkv.shape[1] == 3 * d and d % LANES == 0
    n_pairs = d // LANES
    d1, d2 = DILATIONS[1], DILATIONS[2]
    assert t % (d2 * 16) == 0
    half = ATTN_HEAD_DIM // 2

    def rope_order(cols):
        lead = cols.shape[:-1]
        x = cols.reshape(lead + (n_pairs, 2, 2, half))
        return jnp.swapaxes(x, -3, -2).reshape(lead + (d,))

    w = jnp.concatenate([rope_order(w_qkv[:, :d]), rope_order(w_qkv[:, d:2 * d]), w_qkv[:, 2 * d:]], axis=1)
    q_gain_lanes = rope_order(jnp.tile(q_gain * (ATTN_HEAD_DIM ** -0.5 * LOG2_E), d // ATTN_HEAD_DIM))
    k_gain_lanes = rope_order(jnp.tile(k_gain, d // ATTN_HEAD_DIM))
    group = 2 * LANES
    head_of_lane = _rope_lane_perm(d)[:group] // ATTN_HEAD_DIM
    same_head = (head_of_lane[:, None] == head_of_lane[None, :]).astype(BF16)
    shapes = [(bsz, n_pairs, s, LANES), (bsz, n_pairs, d1, s // d1, LANES), (bsz, n_pairs, d2, s // d2, LANES)]
    n_seq = s // t
    n_tiles = bsz * n_seq
    proj = lambda j: jnp.minimum(j, n_tiles - 1)
    done = lambda j: jnp.maximum(j - 1, 0)
    specs = [
        pl.BlockSpec((1, n_pairs, t, LANES), lambda j: (done(j) // n_seq, 0, done(j) % n_seq, 0)),
        pl.BlockSpec((1, n_pairs, d1, t // d1, LANES), lambda j: (done(j) // n_seq, 0, 0, done(j) % n_seq, 0)),
        pl.BlockSpec((1, n_pairs, d2, t // d2, LANES), lambda j: (done(j) // n_seq, 0, 0, done(j) % n_seq, 0)),
    ]
    outs = pl.pallas_call(
        _qkv_kernel,
        grid=(n_tiles + 1,),
        in_specs=[
            pl.BlockSpec((1, t, d), lambda j: (proj(j) // n_seq, proj(j) % n_seq, 0)),
            _const_spec((1, d)),
            _const_spec((d, 3 * d)),
            _const_spec((1, d)),
            _const_spec((1, d)),
            pl.BlockSpec((t, LANES), lambda j: (done(j) % n_seq, 0)),
            pl.BlockSpec((t, LANES), lambda j: (done(j) % n_seq, 0)),
            _const_spec((group, group)),
        ],
        out_specs=specs * 3,
        out_shape=[jax.ShapeDtypeStruct(sh, BF16) for sh in shapes] * 3,
        scratch_shapes=[pltpu.VMEM((t, 3 * d), F32), pltpu.VMEM((2, t, LANES), F32),
                        pltpu.VMEM((2, t, LANES), F32)],
        compiler_params=pltpu.CompilerParams(
            dimension_semantics=("arbitrary",), vmem_limit_bytes=VMEM_LIMIT_BYTES),
        name="qkv",
    )(h, g.reshape(1, d), w.astype(BF16), q_gain_lanes.reshape(1, d), k_gain_lanes.reshape(1, d), cos, sin,
      same_head)
    return outs[0:3], outs[3:6], outs[6:9]


def _attn_kernel(q1_ref, q4_ref, q16_ref, k1_ref, k4_ref, k16_ref, v1_ref, v4_ref, v16_ref, out_ref,
                 bias_ref, acc_ref, m_ref, l_ref, merged_ref):
    tq = q1_ref.shape[2]
    bq = ATTN_BLOCK_Q
    w = ATTN_WINDOW_STEPS
    nk = bq + w
    t0 = pl.program_id(2) * tq
    d1 = DILATIONS[1]

    def rows_of(ref, res, start, size):
        if len(ref.shape) == 4:
            return ref[0, 0, pl.ds(start, size), :]
        return ref[0, 0, res, pl.ds(start, size), :]

    i_minus_j = (lax.broadcasted_iota(jnp.int32, (bq, nk), 0) - lax.broadcasted_iota(jnp.int32, (bq, nk), 1))
    for variant, koff in enumerate((w, 0)):
        dist = i_minus_j + koff
        bias_ref[variant] = jnp.where((dist >= 0) & (dist <= w), 0.0, MASK_VALUE)

    lane = lax.broadcasted_iota(jnp.int32, (bq, LANES), 1)
    head0 = lane < ATTN_HEAD_DIM
    q_head0 = jnp.where((lane // (ATTN_HEAD_DIM // 2)) % 2 == 0, 1.0, 0.0).astype(BF16)
    q_head1 = (1.0 - q_head0.astype(F32)).astype(BF16)
    ones = jnp.ones((nk, LANES), BF16)

    branches = list(enumerate(
        zip(DILATIONS, (q1_ref, q4_ref, q16_ref), (k1_ref, k4_ref, k16_ref), (v1_ref, v4_ref, v16_ref))))
    n1 = tq // d1
    rows_per_step = ATTN_MERGE_ROWS
    slot_of = {br: i for i, br in enumerate(b for b, dil in enumerate(DILATIONS) if dil != d1)}
    for br, (dil, q_ref, k_ref, v_ref) in branches[2:] + branches[:1] + branches[1:2]:
        blocks_per_res = tq // (dil * bq)
        for res in range(dil):
            for blk in range(blocks_per_res):
                l_k0 = t0 // dil + bq * blk - w
                variant = (l_k0 < 0).astype(jnp.int32)
                row0 = pl.multiple_of(jnp.maximum(l_k0, 0), bq)
                qb = rows_of(q_ref, res, blk * bq, bq)
                kb = rows_of(k_ref, res, row0, nk)
                vb = rows_of(v_ref, res, row0, nk)
                qq = jnp.concatenate([qb * q_head0, qb * q_head1], axis=0)
                s = _dot_nt(qq, kb)
                bias = bias_ref[variant]
                s = jnp.concatenate([s[:bq] + bias, s[bq:] + bias], axis=0)
                m = jnp.max(s, axis=-1, keepdims=True)
                p = jnp.exp2(s - m).astype(BF16)
                pv = _dot(p, jnp.concatenate([vb, ones], axis=1))
                acc = jnp.where(head0, pv[:bq, :LANES], pv[bq:, :LANES])
                l = jnp.where(head0, pv[:bq, LANES:], pv[bq:, LANES:])
                m = jnp.where(head0, m[:bq], m[bq:])
                if dil == 1:
                    rows = pl.ds(blk * bq, bq)
                elif dil != d1:
                    rows = pl.ds((res % d1) * n1 + res // d1 + (dil // d1) * bq * blk, bq, stride=dil // d1)
                if dil != d1:
                    acc_ref[slot_of[br], rows, :] = acc
                    l_ref[slot_of[br], rows, :] = l
                    m_ref[slot_of[br], rows, :] = m
                    continue
                for c0 in range(0, bq, rows_per_step):
                    tok = pl.ds(res + d1 * (blk * bq + c0), rows_per_step, stride=d1)
                    res_rows = pl.ds(res * n1 + blk * bq + c0, rows_per_step)
                    sl = slice(c0, c0 + rows_per_step)
                    parts = [(m[sl], l[sl], acc[sl])]
                    for other in range(len(DILATIONS)):
                        if other != br:
                            src = tok if DILATIONS[other] == 1 else res_rows
                            slot = slot_of[other]
                            parts.append((m_ref[slot, src, :], l_ref[slot, src, :], acc_ref[slot, src, :]))
                    m_all = functools.reduce(jnp.maximum, [part[0] for part in parts])
                    num = jnp.zeros((rows_per_step, LANES), F32)
                    den = jnp.zeros((rows_per_step, LANES), F32)
                    for m_br, l_br, acc_br in parts:
                        wgt = jnp.exp2(m_br - m_all)
                        num = num + wgt * acc_br
                        den = den + wgt * l_br
                    merged_ref[tok, :] = num / den

    out_ref[0] = merged_ref[...].astype(out_ref.dtype)


def _attention(q, k, v):
    bsz, n_pairs, s, _ = q[0].shape
    tq = ATTN_Q_TILE
    assert s % tq == 0 and tq % (ATTN_BLOCK_Q * max(DILATIONS)) == 0
    assert s // max(DILATIONS) >= ATTN_BLOCK_Q + ATTN_WINDOW_STEPS
    assert len(DILATIONS) == 3 and DILATIONS[0] == 1
    nb = len(DILATIONS) - 1

    def spec(dil, rows, whole):
        if dil == 1:
            return pl.BlockSpec((1, 1, rows, LANES), lambda b, hp, i: (b, hp, 0 if whole else i, 0))
        return pl.BlockSpec((1, 1, dil, rows, LANES), lambda b, hp, i: (b, hp, 0, 0 if whole else i, 0))

    q_specs = [spec(dil, tq // dil, False) for dil in DILATIONS]
    kv_specs = [spec(dil, s // dil, True) for dil in DILATIONS]
    return pl.pallas_call(
        _attn_kernel,
        grid=(bsz, n_pairs, s // tq),
        in_specs=q_specs + kv_specs + kv_specs,
        out_specs=pl.BlockSpec((1, tq, LANES), lambda b, hp, i: (b, i, hp)),
        out_shape=jax.ShapeDtypeStruct((bsz, s, n_pairs * LANES), BF16),
        scratch_shapes=[
            pltpu.VMEM((2, ATTN_BLOCK_Q, ATTN_BLOCK_Q + ATTN_WINDOW_STEPS), F32),
        ] + [pltpu.VMEM((nb, tq, LANES), F32)] * 3 + [pltpu.VMEM((tq, LANES), F32)],
        compiler_params=pltpu.CompilerParams(
            dimension_semantics=("arbitrary", "arbitrary", "arbitrary"), vmem_limit_bytes=VMEM_LIMIT_BYTES),
        name="dilated_attention",
    )(*q, *k, *v)


def kernel(x, p, mix_norm, w_in_ab, pool_w, pool_scale, hgrn_lb, hgrn_o_norm, w_out_ab, w_qkv, q_norm, k_norm,
           w_o, mlp_norm, w_up, w_down, ple_norm, w_ple, w_ple_gate):
    depth = p.shape[0]
    s = x.shape[1]
    h = x
    cos = sin = None
    for layer in range(depth):
        if layer % 2 == 0:
            e = layer // 2
            mix = _mixer0(h, mix_norm[layer], w_in_ab[e], pool_w[e], pool_scale[e], hgrn_lb, hgrn_o_norm[e], layer)
            w_mix = w_out_ab[e]
        else:
            ci = layer // 2
            if cos is None:
                cos, sin = _rope_tables(s)
            q, k, v = _qkv(h, mix_norm[layer], w_qkv[ci], q_norm[ci], k_norm[ci], cos, sin)
            mix = _attention(q, k, v)
            w_mix = w_o[ci]
        h = _post(h, mix, p, layer, w_mix, mlp_norm[layer], w_up[layer], w_down[layer], ple_norm[layer],
                  w_ple_gate[layer], w_ple[layer])
    return h
```
